```python
import jax, jax.numpy as jnp
from jax import lax
import numpy as np

D_MODEL = 2048
BATCH = 2
SEQ = 4096
DEPTH = 4

N_MIXERS = 3
EPS = 1e-6
D_FF = 5504
POOL_WINDOWS = (2, 4, 8, 16)
N_POOL_GROUPS = len(POOL_WINDOWS)
POOL_GROUP = D_MODEL // N_POOL_GROUPS
RET_HEADS = 8
RET_DK = D_MODEL // RET_HEADS
RET_DV = 2 * D_MODEL // RET_HEADS
RET_CHUNK = 128
SWA_HEADS = 32
SWA_KV_HEADS = 4
SWA_HD = 64
SWA_GROUP = SWA_HEADS // SWA_KV_HEADS
SWA_WINDOW = 128
SWA_BLOCK = 128
ROPE_THETA = 10000.0
N_POOL_LAYERS = (DEPTH + 2) // 3
N_RET_LAYERS = (DEPTH + 1) // 3
N_SWA_LAYERS = DEPTH // 3

kernel_name = "hybrid_pool_retention_swa_macaron"


def rms_norm(x, g):
    xf = x.astype(jnp.float32)
    y = xf * lax.rsqrt(jnp.mean(xf * xf, axis=-1, keepdims=True) + EPS)
    return (y * g.astype(jnp.float32)).astype(x.dtype)


def apply_rotary(x, positions, inv_freq):
    ang = positions.astype(jnp.float32)[:, :, None] * inv_freq[None, None, :]
    cos = jnp.cos(ang)[:, :, None, :]
    sin = jnp.sin(ang)[:, :, None, :]
    x1, x2 = jnp.split(x.astype(jnp.float32), 2, axis=-1)
    out = jnp.concatenate([x1 * cos - x2 * sin, x2 * cos + x1 * sin], axis=-1)
    return out.astype(x.dtype)


def swiglu(x, w_in, w_out):
    gate, up = jnp.split(x @ w_in, 2, axis=-1)
    return (jax.nn.silu(gate) * up) @ w_out


def pool_mixer(x, w_group, scale):
    B, S, D = x.shape
    xf = x.astype(jnp.float32).reshape(B, S, N_POOL_GROUPS, POOL_GROUP)
    cs = jnp.pad(jnp.cumsum(xf, axis=1), ((0, 0), (1, 0), (0, 0), (0, 0)))
    t = jnp.arange(1, S + 1)[:, None]
    win = jnp.array(POOL_WINDOWS)[None, :]
    lo = jnp.maximum(t - win, 0)
    cs_lo = cs[:, lo, jnp.arange(N_POOL_GROUPS)[None, :]]
    count = jnp.minimum(t, win).astype(jnp.float32)
    pooled = (cs[:, 1:] - cs_lo) / count[None, :, :, None]
    mixed = (pooled - xf).astype(x.dtype)
    y = jnp.einsum('bsgc,gcd->bsgd', mixed, w_group).reshape(B, S, D)
    return y * scale


def retention_mixer(x, positions, w_in, gn_g, gn_b, w_out):
    B, S, D = x.shape
    N = S // RET_CHUNK
    hproj = x @ w_in
    q, k, v, g = jnp.split(hproj, [D, 2 * D, 4 * D], axis=-1)
    inv_freq = ROPE_THETA ** (-jnp.linspace(0.0, 1.0, RET_DK // 2, dtype=jnp.float32))
    q = apply_rotary(q.reshape(B, S, RET_HEADS, RET_DK), positions, inv_freq)
    k = apply_rotary(k.reshape(B, S, RET_HEADS, RET_DK), positions, inv_freq) * (RET_DK ** -0.5)
    v = v.reshape(B, S, RET_HEADS, RET_DV)

    def to_chunks(t):
        return t.astype(jnp.float32).reshape(B, N, RET_CHUNK, RET_HEADS, t.shape[-1]).transpose(1, 0, 3, 2, 4)

    log_gamma = jnp.log(1.0 - 2.0 ** (-5.0 - jnp.arange(RET_HEADS, dtype=jnp.float32)))
    idx = jnp.arange(RET_CHUNK, dtype=jnp.float32)
    rel = idx[:, None] - idx[None, :]
    inner_decay = jnp.where(rel[None] >= 0, jnp.exp(jnp.maximum(rel, 0.0)[None] * log_gamma[:, None, None]), 0.0)
    cross_decay = jnp.exp((idx + 1.0)[None, :] * log_gamma[:, None])
    state_decay = jnp.exp((RET_CHUNK - 1.0 - idx)[None, :] * log_gamma[:, None])
    chunk_decay = jnp.exp(RET_CHUNK * log_gamma)

    def step(state, qkv):
        qc, kc, vc = qkv
        scores = jnp.einsum('bhid,bhjd->bhij', qc, kc) * inner_decay[None]
        inner = jnp.einsum('bhij,bhjv->bhiv', scores, vc)
        cross = jnp.einsum('bhid,bhdv->bhiv', qc, state) * cross_decay[None, :, :, None]
        new_state = state * chunk_decay[None, :, None, None] + jnp.einsum(
            'bhjd,bhjv->bhdv', kc * state_decay[None, :, :, None], vc)
        return new_state, inner + cross

    state0 = jnp.zeros((B, RET_HEADS, RET_DK, RET_DV), jnp.float32)
    _, o = lax.scan(step, state0, (to_chunks(q), to_chunks(k), to_chunks(v)))
    o = o.transpose(1, 0, 3, 2, 4).reshape(B, S, RET_HEADS, RET_DV)
    mu = jnp.mean(o, axis=-1, keepdims=True)
    var = jnp.mean(jnp.square(o - mu), axis=-1, keepdims=True)
    o = (o - mu) * lax.rsqrt(var + EPS)
    o = o * gn_g.astype(jnp.float32).reshape(RET_HEADS, RET_DV) + gn_b.astype(jnp.float32).reshape(RET_HEADS, RET_DV)
    o = o.reshape(B, S, 2 * D).astype(x.dtype)
    return (jax.nn.silu(g) * o) @ w_out


def swa_mixer(x, positions, w_in, b_in, sinks, w_out, b_out):
    B, S, D = x.shape
    NB = S // SWA_BLOCK
    hproj = x @ w_in + b_in
    q, k, v = jnp.split(hproj, [SWA_HEADS * SWA_HD, (SWA_HEADS + SWA_KV_HEADS) * SWA_HD], axis=-1)
    inv_freq = ROPE_THETA ** (-jnp.arange(0, SWA_HD, 2, dtype=jnp.float32) / SWA_HD)
    q = apply_rotary(q.reshape(B, S, SWA_HEADS, SWA_HD), positions, inv_freq)
    k = apply_rotary(k.reshape(B, S, SWA_KV_HEADS, SWA_HD), positions, inv_freq)
    v = v.reshape(B, S, SWA_KV_HEADS, SWA_HD)

    qb = q.reshape(B, NB, SWA_BLOCK, SWA_KV_HEADS, SWA_GROUP, SWA_HD)

    def band(t):
        prev = jnp.pad(t, ((0, 0), (SWA_BLOCK, 0), (0, 0), (0, 0)))[:, :S]
        shp = (B, NB, SWA_BLOCK, SWA_KV_HEADS, SWA_HD)
        return jnp.concatenate([prev.reshape(shp), t.reshape(shp)], axis=2)

    kb, vb = band(k), band(v)
    i = jnp.arange(SWA_BLOCK)[:, None]
    j = jnp.arange(2 * SWA_BLOCK)[None, :]
    nblk = jnp.arange(NB)[:, None, None]
    diff = i + SWA_BLOCK - j
    key_pos = nblk * SWA_BLOCK - SWA_BLOCK + j
    allowed = (diff >= 0) & (diff < SWA_WINDOW) & (key_pos >= 0)

    scores = jnp.einsum('bnikgd,bnjkd->bnkgij', qb.astype(jnp.float32), kb.astype(jnp.float32)) * (SWA_HD ** -0.5)
    scores = jnp.where(allowed[None, :, None, None], scores, -jnp.inf)
    sink = jnp.broadcast_to(sinks.astype(jnp.float32).reshape(1, 1, SWA_KV_HEADS, SWA_GROUP, 1, 1),
                            scores.shape[:-1] + (1,))
    probs = jax.nn.softmax(jnp.concatenate([scores, sink], axis=-1), axis=-1)[..., :-1]
    out = jnp.einsum('bnkgij,bnjkd->bnikgd', probs, vb.astype(jnp.float32)).astype(x.dtype)
    out = out.reshape(B, S, SWA_HEADS * SWA_HD)
    return out @ w_out + b_out


def setup_inputs(seed: int = 0) -> dict:
    key = jax.random.key(seed)
    ks = jax.random.split(key, 24)
    f32 = jnp.float32

    def nrm(k, shape, fan_in):
        return jax.random.normal(k, shape, f32) * (fan_in ** -0.5)

    def gain(k, shape):
        return 1.0 + 0.05 * jax.random.normal(k, shape, f32)

    x = jax.random.normal(ks[0], (BATCH, SEQ, D_MODEL), f32)
    offsets = jax.random.randint(ks[1], (BATCH, 1), 0, 1024, dtype=jnp.int32)
    positions = offsets + jnp.arange(SEQ, dtype=jnp.int32)[None, :]
    swa_in = (SWA_HEADS + 2 * SWA_KV_HEADS) * SWA_HD
    return {
        "x": x,
        "positions": positions,
        "ln_ffn1": gain(ks[2], (DEPTH, 2, D_MODEL)),
        "ln_mix": gain(ks[3], (DEPTH, 2, D_MODEL)),
        "ln_ffn2": gain(ks[4], (DEPTH, 2, D_MODEL)),
        "ffn_w_in": nrm(ks[5], (DEPTH, 2, D_MODEL, 2 * D_FF), D_MODEL),
        "ffn_w_out": nrm(ks[6], (DEPTH, 2, D_FF, D_MODEL), D_FF),
        "pool_w": nrm(ks[7], (N_POOL_LAYERS, N_POOL_GROUPS, POOL_GROUP, POOL_GROUP), POOL_GROUP),
        "pool_scale": 1.0 + 0.1 * jax.random.normal(ks[8], (N_POOL_LAYERS, D_MODEL), f32),
        "ret_w_in": nrm(ks[9], (N_RET_LAYERS, D_MODEL, 6 * D_MODEL), D_MODEL),
        "ret_gn_g": gain(ks[10], (N_RET_LAYERS, 2 * D_MODEL)),
        "ret_gn_b": 0.02 * jax.random.normal(ks[11], (N_RET_LAYERS, 2 * D_MODEL), f32),
        "ret_w_out": nrm(ks[12], (N_RET_LAYERS, 2 * D_MODEL, D_MODEL), 2 * D_MODEL),
        "swa_w_in": nrm(ks[13], (N_SWA_LAYERS, D_MODEL, swa_in), D_MODEL),
        "swa_b_in": 0.02 * jax.random.normal(ks[14], (N_SWA_LAYERS, swa_in), f32),
        "swa_sinks": 0.5 * jax.random.normal(ks[15], (N_SWA_LAYERS, SWA_HEADS), f32),
        "swa_w_out": nrm(ks[16], (N_SWA_LAYERS, SWA_HEADS * SWA_HD, D_MODEL), SWA_HEADS * SWA_HD),
        "swa_b_out": 0.02 * jax.random.normal(ks[17], (N_SWA_LAYERS, D_MODEL), f32),
    }


def reference(x, positions, ln_ffn1, ln_mix, ln_ffn2, ffn_w_in, ffn_w_out,
              pool_w, pool_scale, ret_w_in, ret_gn_g, ret_gn_b, ret_w_out,
              swa_w_in, swa_b_in, swa_sinks, swa_w_out, swa_b_out):
    h = x
    for i in range(DEPTH):
        f = swiglu(rms_norm(h, ln_ffn1[i, 0]), ffn_w_in[i, 0], ffn_w_out[i, 0])
        h = h + 0.5 * rms_norm(f, ln_ffn1[i, 1])
        u = rms_norm(h, ln_mix[i, 0])
        kind, j = i % N_MIXERS, i // N_MIXERS
        if kind == 0:
            m = pool_mixer(u, pool_w[j], pool_scale[j])
        elif kind == 1:
            m = retention_mixer(u, positions, ret_w_in[j], ret_gn_g[j], ret_gn_b[j], ret_w_out[j])
        else:
            m = swa_mixer(u, positions, swa_w_in[j], swa_b_in[j], swa_sinks[j], swa_w_out[j], swa_b_out[j])
        h = h + rms_norm(m, ln_mix[i, 1])
        f = swiglu(rms_norm(h, ln_ffn2[i, 0]), ffn_w_in[i, 1], ffn_w_out[i, 1])
        h = h + 0.5 * rms_norm(f, ln_ffn2[i, 1])
    return h
```

```python
import functools

import jax
import jax.numpy as jnp
from jax import lax
from jax.experimental import pallas as pl
from jax.experimental.pallas import tpu as pltpu

F32 = jnp.float32
BF16 = jnp.bfloat16

D_MODEL = 2048
DEPTH = 4
N_MIXERS = 3
EPS = 1e-6
D_FF = 5504
POOL_WINDOWS = (2, 4, 8, 16)
POOL_GROUP = D_MODEL // len(POOL_WINDOWS)
POOL_HALO = 16
RET_HEADS = 8
RET_DK = D_MODEL // RET_HEADS
RET_DV = 2 * D_MODEL // RET_HEADS
SWA_HEADS = 32
SWA_KV_HEADS = 4
SWA_HD = 64
SWA_GROUP = SWA_HEADS // SWA_KV_HEADS
SWA_WINDOW = 128
SWA_BLOCK = 128
ROPE_THETA = 10000.0

LANES = 128
VMEM_LIMIT_BYTES = 58 * 1024 * 1024

ROW_TILE = 1024
NORM_ROWS = 128
FF_TILE = 512
D_FF_PAD = -(-D_FF // FF_TILE) * FF_TILE
OUT_COLS = 512
PROJ_OUT_K = 1024
POOL_ROWS = 256
RET_COL_TILE = 1024
RET_CHUNK = 256


def _rms(x, g):
    return x * lax.rsqrt(jnp.mean(x * x, axis=-1, keepdims=True) + EPS) * g


def _params(*sem):
    return pltpu.CompilerParams(dimension_semantics=sem, vmem_limit_bytes=VMEM_LIMIT_BYTES)


def _norm_rows_to(dst_ref, src_ref, g, rows):
    def body(i, carry):
        r = pl.ds(pl.multiple_of(i * NORM_ROWS, NORM_ROWS), NORM_ROWS)
        dst_ref[r, :] = _rms(src_ref[r, :], g).astype(dst_ref.dtype)
        return carry
    lax.fori_loop(0, rows // NORM_ROWS, body, 0)


def _residual_norm_rows(o_ref, h_ref, g, weight, rows):
    def body(i, carry):
        r = pl.ds(pl.multiple_of(i * NORM_ROWS, NORM_ROWS), NORM_ROWS)
        o_ref[r, :] = h_ref[r, :] + weight * _rms(o_ref[r, :], g)
        return carry
    lax.fori_loop(0, rows // NORM_ROWS, body, 0)


def _ffn_kernel(h_ref, g1_ref, g2_ref, wg_ref, wu_ref, wo_ref, o_ref, xn_ref):
    f = pl.program_id(1)
    rows = h_ref.shape[0]

    @pl.when(f == 0)
    def _():
        _norm_rows_to(xn_ref, h_ref, g1_ref[...], rows)

    xn = xn_ref[...]
    gate = jnp.dot(xn, wg_ref[...], preferred_element_type=F32)
    up = jnp.dot(xn, wu_ref[...], preferred_element_type=F32)
    hid = (jax.nn.silu(gate) * up).astype(BF16)
    for c in range(D_MODEL // OUT_COLS):
        cols = slice(c * OUT_COLS, (c + 1) * OUT_COLS)
        part = jnp.dot(hid, wo_ref[:, cols], preferred_element_type=F32)

        @pl.when(f == 0)
        def _():
            o_ref[:, cols] = part

        @pl.when(f > 0)
        def _():
            o_ref[:, cols] += part

    @pl.when(f == pl.num_programs(1) - 1)
    def _():
        _residual_norm_rows(o_ref, h_ref, g2_ref[...], 0.5, rows)


def _ffn(h, ln, w_in, w_out):
    m = h.shape[0]
    pad = D_FF_PAD - D_FF
    wg = jnp.pad(w_in[:, :D_FF], ((0, 0), (0, pad))).astype(BF16)
    wu = jnp.pad(w_in[:, D_FF:], ((0, 0), (0, pad))).astype(BF16)
    wo = jnp.pad(w_out, ((0, pad), (0, 0))).astype(BF16)
    return pl.pallas_call(
        _ffn_kernel,
        out_shape=jax.ShapeDtypeStruct((m, D_MODEL), F32),
        grid=(m // ROW_TILE, D_FF_PAD // FF_TILE),
        in_specs=[
            pl.BlockSpec((ROW_TILE, D_MODEL), lambda i, f: (i, 0)),
            pl.BlockSpec((1, D_MODEL), lambda i, f: (0, 0)),
            pl.BlockSpec((1, D_MODEL), lambda i, f: (0, 0)),
            pl.BlockSpec((D_MODEL, FF_TILE), lambda i, f: (0, f)),
            pl.BlockSpec((D_MODEL, FF_TILE), lambda i, f: (0, f)),
            pl.BlockSpec((FF_TILE, D_MODEL), lambda i, f: (f, 0)),
        ],
        out_specs=pl.BlockSpec((ROW_TILE, D_MODEL), lambda i, f: (i, 0)),
        scratch_shapes=[pltpu.VMEM((ROW_TILE, D_MODEL), BF16)],
        compiler_params=_params("parallel", "arbitrary"),
        name="ffn",
    )(h, ln[0:1], ln[1:2], wg, wu, wo)


def _pool_kernel(h_ref, halo_ref, gin_ref, gout_ref, w_ref, scale_ref, o_ref):
    t = pl.program_id(1)
    rows = h_ref.shape[1]
    h = h_ref[0]
    gin = gin_ref[...]
    u = _rms(h, gin)
    has_prev = (t > 0).astype(F32)
    halo = _rms(halo_ref[0], gin) * has_prev
    hist = jnp.concatenate([halo, u], axis=0)
    hist_hi = hist.astype(BF16)
    hist_lo = (hist - hist_hi.astype(F32)).astype(BF16)
    qi = lax.broadcasted_iota(jnp.int32, (rows, rows + POOL_HALO), 0) + POOL_HALO
    kj = lax.broadcasted_iota(jnp.int32, (rows, rows + POOL_HALO), 1)
    back = qi - kj
    seq_pos = t * rows + lax.broadcasted_iota(jnp.int32, (rows, 1), 0)
    outs = []
    for g, win in enumerate(POOL_WINDOWS):
        cols = slice(g * POOL_GROUP, (g + 1) * POOL_GROUP)
        band = jnp.where((back >= 1) & (back < win), 1.0, 0.0).astype(BF16)
        others = (jnp.dot(band, hist_hi[:, cols], preferred_element_type=F32)
                  + jnp.dot(band, hist_lo[:, cols], preferred_element_type=F32))
        count = jnp.minimum(seq_pos + 1, win).astype(F32)
        own = u[:, cols]
        mixed = ((others + own) / count - own).astype(BF16)
        y = jnp.dot(mixed, w_ref[g], preferred_element_type=F32)
        outs.append(y * scale_ref[:, cols])
    y = jnp.concatenate(outs, axis=1)
    o_ref[0] = h + _rms(y, gout_ref[...])


def _pool_mixer(h, ln, w_group, scale, batch, seq):
    h3 = h.reshape(batch, seq, D_MODEL)
    halo_blocks = POOL_ROWS // POOL_HALO
    out = pl.pallas_call(
        _pool_kernel,
        out_shape=jax.ShapeDtypeStruct(h3.shape, F32),
        grid=(batch, seq // POOL_ROWS),
        in_specs=[
            pl.BlockSpec((1, POOL_ROWS, D_MODEL), lambda b, t: (b, t, 0)),
            pl.BlockSpec((1, POOL_HALO, D_MODEL),
                         lambda b, t: (b, jnp.maximum(t * halo_blocks - 1, 0), 0)),
            pl.BlockSpec((1, D_MODEL), lambda b, t: (0, 0)),
            pl.BlockSpec((1, D_MODEL), lambda b, t: (0, 0)),
            pl.BlockSpec((len(POOL_WINDOWS), POOL_GROUP, POOL_GROUP), lambda b, t: (0, 0, 0)),
            pl.BlockSpec((1, D_MODEL), lambda b, t: (0, 0)),
        ],
        out_specs=pl.BlockSpec((1, POOL_ROWS, D_MODEL), lambda b, t: (b, t, 0)),
        compiler_params=_params("parallel", "parallel"),
        name="pool_mixer",
    )(h3, h3, ln[0:1], ln[1:2], w_group.astype(BF16), scale.reshape(1, D_MODEL))
    return out.reshape(batch * seq, D_MODEL)


def _rope_pairs(y, cos, sin):
    outs = []
    for hb in range(y.shape[1] // (2 * LANES)):
        x1 = y[:, (2 * hb) * LANES:(2 * hb + 1) * LANES]
        x2 = y[:, (2 * hb + 1) * LANES:(2 * hb + 2) * LANES]
        outs += [x1 * cos - x2 * sin, x2 * cos + x1 * sin]
    return jnp.concatenate(outs, axis=1)


def _rope_in_lane(y, cos, sin_signed, first_half):
    half = SWA_HD // 2
    outs = []
    for c in range(y.shape[1] // LANES):
        xb = y[:, c * LANES:(c + 1) * LANES]
        partner = jnp.where(first_half, pltpu.roll(xb, LANES - half, axis=1), pltpu.roll(xb, half, axis=1))
        outs.append(xb * cos + partner * sin_signed)
    return jnp.concatenate(outs, axis=1)


def _proj_in_kernel(h_ref, g_ref, pos_ref, freq_ref, w_ref, b_ref, o_ref, xn_ref, cos_ref, sin_ref,
                    *, mode, rope_tiles, scales):
    j = pl.program_id(1)
    rows = h_ref.shape[0]

    @pl.when(j == 0)
    def _():
        _norm_rows_to(xn_ref, h_ref, g_ref[...], rows)
        ang = pos_ref[...].astype(F32) * freq_ref[...]
        cos_ref[...] = jnp.cos(ang)
        sin_ref[...] = jnp.sin(ang)

    y = jnp.dot(xn_ref[...], w_ref[...], preferred_element_type=F32) + b_ref[...]

    lane = lax.broadcasted_iota(jnp.int32, (1, LANES), 1)
    first_half = (lane % SWA_HD) < (SWA_HD // 2)

    def roped():
        cos = cos_ref[...]
        sin = sin_ref[...]
        if mode == "pairs":
            return _rope_pairs(y, cos, sin)
        return _rope_in_lane(y, cos, jnp.where(first_half, -sin, sin), first_half)

    for start, stop, factor in scales:
        @pl.when((j >= start) & (j < stop))
        def _(factor=factor):
            o_ref[...] = (roped() * factor).astype(o_ref.dtype)

    @pl.when(j >= rope_tiles)
    def _():
        o_ref[...] = y.astype(o_ref.dtype)


def _proj_in(h, g, pos, freq_row, w, bias, *, col_tile, mode, rope_tiles, scales):
    m = h.shape[0]
    n = w.shape[1]
    kern = functools.partial(_proj_in_kernel, mode=mode, rope_tiles=rope_tiles, scales=scales)
    return pl.pallas_call(
        kern,
        out_shape=jax.ShapeDtypeStruct((m, n), BF16),
        grid=(m // ROW_TILE, n // col_tile),
        in_specs=[
            pl.BlockSpec((ROW_TILE, D_MODEL), lambda i, j: (i, 0)),
            pl.BlockSpec((1, D_MODEL), lambda i, j: (0, 0)),
            pl.BlockSpec((ROW_TILE, 1), lambda i, j: (i, 0)),
            pl.BlockSpec((1, LANES), lambda i, j: (0, 0)),
            pl.BlockSpec((D_MODEL, col_tile), lambda i, j: (0, j)),
            pl.BlockSpec((1, col_tile), lambda i, j: (0, j)),
        ],
        out_specs=pl.BlockSpec((ROW_TILE, col_tile), lambda i, j: (i, j)),
        scratch_shapes=[pltpu.VMEM((ROW_TILE, D_MODEL), BF16),
                        pltpu.VMEM((ROW_TILE, LANES), F32),
                        pltpu.VMEM((ROW_TILE, LANES), F32)],
        compiler_params=_params("parallel", "arbitrary"),
        name="proj_in_" + mode,
    )(h, g, pos, freq_row, w.astype(BF16), bias.reshape(1, n))


def _proj_out_kernel(x_ref, w_ref, b_ref, g_ref, h_ref, o_ref):
    k = pl.program_id(1)
    rows = h_ref.shape[0]
    x = x_ref[...]
    for c in range(D_MODEL // OUT_COLS):
        cols = slice(c * OUT_COLS, (c + 1) * OUT_COLS)
        part = jnp.dot(x, w_ref[:, cols], preferred_element_type=F32)

        @pl.when(k == 0)
        def _():
            o_ref[:, cols] = part + b_ref[:, cols]

        @pl.when(k > 0)
        def _():
            o_ref[:, cols] += part

    @pl.when(k == pl.num_programs(1) - 1)
    def _():
        _residual_norm_rows(o_ref, h_ref, g_ref[...], 1.0, rows)


def _proj_out(x, w, bias, g, h):
    m, kdim = x.shape
    return pl.pallas_call(
        _proj_out_kernel,
        out_shape=jax.ShapeDtypeStruct((m, D_MODEL), F32),
        grid=(m // ROW_TILE, kdim // PROJ_OUT_K),
        in_specs=[
            pl.BlockSpec((ROW_TILE, PROJ_OUT_K), lambda i, k: (i, k)),
            pl.BlockSpec((PROJ_OUT_K, D_MODEL), lambda i, k: (k, 0)),
            pl.BlockSpec((1, D_MODEL), lambda i, k: (0, 0)),
            pl.BlockSpec((1, D_MODEL), lambda i, k: (0, 0)),
            pl.BlockSpec((ROW_TILE, D_MODEL), lambda i, k: (i, 0)),
        ],
        out_specs=pl.BlockSpec((ROW_TILE, D_MODEL), lambda i, k: (i, 0)),
        compiler_params=_params("parallel", "arbitrary"),
        name="proj_out",
    )(x, w.astype(BF16), bias.reshape(1, D_MODEL), g, h)


def _retention_kernel(lg_ref, q_ref, k_ref, v_ref, gate_ref, gng_ref, gnb_ref, o_ref, state_ref):
    head = pl.program_id(1)
    chunk = q_ref.shape[1]

    @pl.when(pl.program_id(2) == 0)
    def _():
        state_ref[...] = jnp.zeros_like(state_ref)

    log_gamma = jnp.full((1, 1), lg_ref[head], F32)
    q = q_ref[0]
    k = k_ref[0]
    v = v_ref[0]
    qi = lax.broadcasted_iota(jnp.int32, (chunk, chunk), 0)
    kj = lax.broadcasted_iota(jnp.int32, (chunk, chunk), 1)
    rel = (qi - kj).astype(F32)
    inner_decay = jnp.where(rel >= 0, jnp.exp(jnp.maximum(rel, 0.0) * log_gamma), 0.0)
    idx = lax.broadcasted_iota(jnp.int32, (chunk, 1), 0).astype(F32)
    cross_decay = jnp.exp((idx + 1.0) * log_gamma)
    state_decay = jnp.exp((chunk - 1.0 - idx) * log_gamma)
    chunk_decay = jnp.exp(float(chunk) * log_gamma)

    scores = lax.dot_general(q, k, (((1,), (1,)), ((), ())), preferred_element_type=F32) * inner_decay
    inner = jnp.dot(scores.astype(BF16), v, preferred_element_type=F32)
    state = state_ref[...]
    cross = jnp.dot(q, state.astype(BF16), preferred_element_type=F32) * cross_decay
    k_decayed = (k.astype(F32) * state_decay).astype(BF16)
    state_ref[...] = state * chunk_decay + lax.dot_general(
        k_decayed, v, (((0,), (0,)), ((), ())), preferred_element_type=F32)

    o = inner + cross
    mu = jnp.mean(o, axis=-1, keepdims=True)
    var = jnp.mean(jnp.square(o - mu), axis=-1, keepdims=True)
    o = (o - mu) * lax.rsqrt(var + EPS)
    o = o * gng_ref[...] + gnb_ref[...]
    o_ref[0] = (jax.nn.silu(gate_ref[0].astype(F32)) * o).astype(o_ref.dtype)


def _retention_core(hproj, gn_g, gn_b, batch, seq):
    hp = hproj.reshape(batch, seq, 6 * D_MODEL)
    log_gamma = jnp.log(1.0 - 2.0 ** (-5.0 - jnp.arange(RET_HEADS, dtype=F32)))
    k_blk0 = D_MODEL // RET_DK
    v_blk0 = 2 * D_MODEL // RET_DV
    g_blk0 = 4 * D_MODEL // RET_DV
    out = pl.pallas_call(
        _retention_kernel,
        out_shape=jax.ShapeDtypeStruct((batch, seq, 2 * D_MODEL), BF16),
        grid=(batch, RET_HEADS, seq // RET_CHUNK),
        in_specs=[
            pl.BlockSpec(memory_space=pltpu.SMEM),
            pl.BlockSpec((1, RET_CHUNK, RET_DK), lambda b, hd, c: (b, c, hd)),
            pl.BlockSpec((1, RET_CHUNK, RET_DK), lambda b, hd, c: (b, c, k_blk0 + hd)),
            pl.BlockSpec((1, RET_CHUNK, RET_DV), lambda b, hd, c: (b, c, v_blk0 + hd)),
            pl.BlockSpec((1, RET_CHUNK, RET_DV), lambda b, hd, c: (b, c, g_blk0 + hd)),
            pl.BlockSpec((1, RET_DV), lambda b, hd, c: (0, hd)),
            pl.BlockSpec((1, RET_DV), lambda b, hd, c: (0, hd)),
        ],
        out_specs=pl.BlockSpec((1, RET_CHUNK, RET_DV), lambda b, hd, c: (b, c, hd)),
        scratch_shapes=[pltpu.VMEM((RET_DK, RET_DV), F32)],
        compiler_params=_params("parallel", "parallel", "arbitrary"),
        name="retention_core",
    )(log_gamma, hp, hp, hp, hp, gn_g.reshape(1, 2 * D_MODEL), gn_b.reshape(1, 2 * D_MODEL))
    return out.reshape(batch * seq, 2 * D_MODEL)


def _swa_kernel(sink_ref, q_ref, kc_ref, vc_ref, kp_ref, vp_ref, o_ref):
    n = pl.program_id(1)
    blk = SWA_BLOCK
    lane = lax.broadcasted_iota(jnp.int32, (1, LANES), 1)
    low = lane < SWA_HD

    def both_halves(t, odd):
        t = t.astype(F32)
        keep = jnp.logical_not(low) if odd else low
        return jnp.where(keep, t, pltpu.roll(t, SWA_HD, axis=1)).astype(BF16)

    k_all = jnp.concatenate([kp_ref[0], kc_ref[0]], axis=0)
    v_all = jnp.concatenate([vp_ref[0], vc_ref[0]], axis=0)
    qi = lax.broadcasted_iota(jnp.int32, (blk, 2 * blk), 0)
    kj = lax.broadcasted_iota(jnp.int32, (blk, 2 * blk), 1)
    diff = qi + blk - kj
    first_key = jnp.where(n > 0, 0, blk)
    allowed = (diff >= 0) & (diff < SWA_WINDOW) & (kj >= first_key)
    allowed = jnp.concatenate([allowed] * SWA_GROUP, axis=0)

    for kv in range(SWA_KV_HEADS):
        kv_cols = slice((kv // 2) * LANES, (kv // 2 + 1) * LANES)
        k2 = both_halves(k_all[:, kv_cols], kv % 2)
        v2 = both_halves(v_all[:, kv_cols], kv % 2)
        q_rows = []
        sink_rows = []
        for jh in range(SWA_GROUP):
            head = kv * SWA_GROUP + jh
            qp = q_ref[0, :, (head // 2) * LANES:(head // 2 + 1) * LANES].astype(F32)
            keep = jnp.logical_not(low) if head % 2 else low
            q_rows.append(jnp.where(keep, qp, 0.0).astype(BF16))
            sink_rows.append(jnp.full((blk, 1), sink_ref[head], F32))
        q_stack = jnp.concatenate(q_rows, axis=0)
        sink = jnp.concatenate(sink_rows, axis=0)
        s = lax.dot_general(q_stack, k2, (((1,), (1,)), ((), ())), preferred_element_type=F32)
        s = jnp.where(allowed, s, -jnp.inf)
        m = jnp.maximum(jnp.max(s, axis=-1, keepdims=True), sink)
        e = jnp.exp(s - m)
        denom = jnp.sum(e, axis=-1, keepdims=True) + jnp.exp(sink - m)
        p = (e / denom).astype(BF16)
        ov = jnp.dot(p, v2, preferred_element_type=F32)
        for jp in range(SWA_GROUP // 2):
            even = ov[(2 * jp) * blk:(2 * jp + 1) * blk]
            odd = ov[(2 * jp + 1) * blk:(2 * jp + 2) * blk]
            c = kv * (SWA_GROUP // 2) + jp
            o_ref[0, :, c * LANES:(c + 1) * LANES] = jnp.where(low, even, odd).astype(o_ref.dtype)


def _swa_core(hproj, sinks, batch, seq):
    width = hproj.shape[1]
    hp = hproj.reshape(batch, seq, width)
    q_width = SWA_HEADS * SWA_HD
    kv_width = SWA_KV_HEADS * SWA_HD
    k_blk = q_width // kv_width
    v_blk = k_blk + 1
    prev = lambda n: jnp.maximum(n - 1, 0)
    out = pl.pallas_call(
        _swa_kernel,
        out_shape=jax.ShapeDtypeStruct((batch, seq, q_width), BF16),
        grid=(batch, seq // SWA_BLOCK),
        in_specs=[
            pl.BlockSpec(memory_space=pltpu.SMEM),
            pl.BlockSpec((1, SWA_BLOCK, q_width), lambda b, n: (b, n, 0)),
            pl.BlockSpec((1, SWA_BLOCK, kv_width), lambda b, n: (b, n, k_blk)),
            pl.BlockSpec((1, SWA_BLOCK, kv_width), lambda b, n: (b, n, v_blk)),
            pl.BlockSpec((1, SWA_BLOCK, kv_width), lambda b, n: (b, prev(n), k_blk)),
            pl.BlockSpec((1, SWA_BLOCK, kv_width), lambda b, n: (b, prev(n), v_blk)),
        ],
        out_specs=pl.BlockSpec((1, SWA_BLOCK, q_width), lambda b, n: (b, n, 0)),
        compiler_params=_params("parallel", "parallel"),
        name="swa_core",
    )(sinks, hp, hp, hp, hp, hp)
    return out.reshape(batch * seq, q_width)


def _retention_mixer(h, ln, pos, w_in, gn_g, gn_b, w_out, batch, seq):
    inv_freq = ROPE_THETA ** (-jnp.linspace(0.0, 1.0, RET_DK // 2, dtype=F32))
    q_tiles = D_MODEL // RET_COL_TILE
    hproj = _proj_in(
        h, ln[0:1], pos, inv_freq.reshape(1, LANES), w_in, jnp.zeros((w_in.shape[1],), F32),
        col_tile=RET_COL_TILE, mode="pairs", rope_tiles=2 * q_tiles,
        scales=((0, q_tiles, 1.0), (q_tiles, 2 * q_tiles, RET_DK ** -0.5)))
    gated = _retention_core(hproj, gn_g, gn_b, batch, seq)
    return _proj_out(gated, w_out, jnp.zeros((D_MODEL,), F32), ln[1:2], h)


def _swa_mixer(h, ln, pos, w_in, b_in, sinks, w_out, b_out, batch, seq):
    inv_freq = ROPE_THETA ** (-jnp.arange(0, SWA_HD, 2, dtype=F32) / SWA_HD)
    freq_row = jnp.tile(inv_freq, LANES // (SWA_HD // 2)).reshape(1, LANES)
    col_tile = SWA_KV_HEADS * SWA_HD
    q_tiles = SWA_HEADS * SWA_HD // col_tile
    hproj = _proj_in(
        h, ln[0:1], pos, freq_row, w_in, b_in,
        col_tile=col_tile, mode="lane", rope_tiles=q_tiles + 1,
        scales=((0, q_tiles, SWA_HD ** -0.5), (q_tiles, q_tiles + 1, 1.0)))
    attn = _swa_core(hproj, sinks, batch, seq)
    return _proj_out(attn, w_out, b_out, ln[1:2], h)


def kernel(x, positions, ln_ffn1, ln_mix, ln_ffn2, ffn_w_in, ffn_w_out, pool_w, pool_scale, ret_w_in,
           ret_gn_g, ret_gn_b, ret_w_out, swa_w_in, swa_b_in, swa_sinks, swa_w_out, swa_b_out):
    batch, seq, _ = x.shape
    h = x.reshape(batch * seq, D_MODEL)
    pos = positions.reshape(batch * seq, 1)
    for i in range(DEPTH):
        h = _ffn(h, ln_ffn1[i], ffn_w_in[i, 0], ffn_w_out[i, 0])
        kind, j = i % N_MIXERS, i // N_MIXERS
        if kind == 0:
            h = _pool_mixer(h, ln_mix[i], pool_w[j], pool_scale[j], batch, seq)
        elif kind == 1:
            h = _retention_mixer(h, ln_mix[i], pos, ret_w_in[j], ret_gn_g[j], ret_gn_b[j], ret_w_out[j],
                                 batch, seq)
        else:
            h = _swa_mixer(h, ln_mix[i], pos, swa_w_in[j], swa_b_in[j], swa_sinks[j], swa_w_out[j],
                           swa_b_out[j], batch, seq)
        h = _ffn(h, ln_ffn2[i], ffn_w_in[i, 1], ffn_w_out[i, 1])
    return h.reshape(batch, seq, D_MODEL)
```

```python
import functools

import jax
import jax.numpy as jnp
from jax import lax
from jax.experimental import pallas as pl
from jax.experimental.pallas import tpu as pltpu

F32 = jnp.float32
BF16 = jnp.bfloat16

D_MODEL = 2048
DEPTH = 4
N_MIXERS = 3
EPS = 1e-6
D_FF = 5504
POOL_WINDOWS = (2, 4, 8, 16)
POOL_GROUP = D_MODEL // len(POOL_WINDOWS)
POOL_HALO = 16
RET_HEADS = 8
RET_DK = D_MODEL // RET_HEADS
RET_DV = 2 * D_MODEL // RET_HEADS
SWA_HEADS = 32
SWA_KV_HEADS = 4
SWA_HD = 64
SWA_GROUP = SWA_HEADS // SWA_KV_HEADS
SWA_WINDOW = 128
SWA_BLOCK = 128
ROPE_THETA = 10000.0

LANES = 128
VMEM_LIMIT_BYTES = 58 * 1024 * 1024

ROW_TILE = 1024
NORM_ROWS = 128
FF_TILE = 512
OUT_COLS = 512
PROJ_OUT_K = 1024
POOL_ROWS = 256
RET_COL_TILE = 1024
RET_CHUNK = 256


def _rms(x, g):
    return x * lax.rsqrt(jnp.mean(x * x, axis=-1, keepdims=True) + EPS) * g


def _params(*sem):
    return pltpu.CompilerParams(dimension_semantics=sem, vmem_limit_bytes=VMEM_LIMIT_BYTES)


def _norm_rows_to(dst_ref, src_ref, g, rows, zero_ref=None):
    def body(i, carry):
        r = pl.ds(pl.multiple_of(i * NORM_ROWS, NORM_ROWS), NORM_ROWS)
        dst_ref[r, :] = _rms(src_ref[r, :], g).astype(dst_ref.dtype)
        if zero_ref is not None:
            zero_ref[r, :] = jnp.zeros((NORM_ROWS, zero_ref.shape[1]), zero_ref.dtype)
        return carry
    lax.fori_loop(0, rows // NORM_ROWS, body, 0)


def _residual_norm_rows(o_ref, h_ref, g, rows):
    def body(i, carry):
        r = pl.ds(pl.multiple_of(i * NORM_ROWS, NORM_ROWS), NORM_ROWS)
        o_ref[r, :] = h_ref[r, :] + _rms(o_ref[r, :], g)
        return carry
    lax.fori_loop(0, rows // NORM_ROWS, body, 0)


def _ff_offset(f, base=0):
    return LANES * (base // LANES + jnp.minimum(f * (FF_TILE // LANES), (D_FF - FF_TILE) // LANES))


def _ffn_kernel(h_ref, g1_ref, g2_ref, wg_ref, wu_ref, wo_ref, o_ref, xn_ref):
    f = pl.program_id(1)
    rows = h_ref.shape[0]

    @pl.when(f == 0)
    def _():
        _norm_rows_to(xn_ref, h_ref, g1_ref[...], rows, zero_ref=o_ref)

    xn = xn_ref[...]
    gate = jnp.dot(xn, wg_ref[...], preferred_element_type=F32)
    up = jnp.dot(xn, wu_ref[...], preferred_element_type=F32)
    repeated = f * FF_TILE - _ff_offset(f)
    fresh = lax.broadcasted_iota(jnp.int32, (1, FF_TILE), 1) >= repeated
    hid = jnp.where(fresh, jax.nn.silu(gate) * up, 0.0).astype(BF16)
    for c in range(D_MODEL // OUT_COLS):
        cols = slice(c * OUT_COLS, (c + 1) * OUT_COLS)
        o_ref[:, cols] += jnp.dot(hid, wo_ref[:, cols], preferred_element_type=F32)

    @pl.when(f == pl.num_programs(1) - 1)
    def _():
        _residual_norm_rows(o_ref, h_ref, 0.5 * g2_ref[...], rows)


def _ffn(h, ln, w_in_all, w_out_all, layer, half):
    m = h.shape[0]
    elem = pl.Element
    return pl.pallas_call(
        _ffn_kernel,
        out_shape=jax.ShapeDtypeStruct((m, D_MODEL), F32),
        grid=(m // ROW_TILE, pl.cdiv(D_FF, FF_TILE)),
        in_specs=[
            pl.BlockSpec((ROW_TILE, D_MODEL), lambda i, f: (i, 0)),
            pl.BlockSpec((1, D_MODEL), lambda i, f: (0, 0)),
            pl.BlockSpec((1, D_MODEL), lambda i, f: (0, 0)),
            pl.BlockSpec((None, None, elem(D_MODEL), elem(FF_TILE)),
                         lambda i, f: (layer, half, 0, _ff_offset(f))),
            pl.BlockSpec((None, None, elem(D_MODEL), elem(FF_TILE)),
                         lambda i, f: (layer, half, 0, _ff_offset(f, D_FF))),
            pl.BlockSpec((None, None, elem(FF_TILE), elem(D_MODEL)),
                         lambda i, f: (layer, half, _ff_offset(f), 0)),
        ],
        out_specs=pl.BlockSpec((ROW_TILE, D_MODEL), lambda i, f: (i, 0)),
        scratch_shapes=[pltpu.VMEM((ROW_TILE, D_MODEL), BF16)],
        compiler_params=_params("parallel", "arbitrary"),
        name="ffn",
    )(h, ln[0:1], ln[1:2], w_in_all, w_in_all, w_out_all)


def _pool_kernel(h_ref, halo_ref, gin_ref, gout_ref, w_ref, scale_ref, o_ref):
    t = pl.program_id(1)
    rows = h_ref.shape[1]
    h = h_ref[0]
    gin = gin_ref[...]
    u = _rms(h, gin)
    has_prev = (t > 0).astype(F32)
    halo = _rms(halo_ref[0], gin) * has_prev
    hist = jnp.concatenate([halo, u], axis=0)
    hist_hi = hist.astype(BF16)
    hist_lo = (hist - hist_hi.astype(F32)).astype(BF16)
    qi = lax.broadcasted_iota(jnp.int32, (rows, rows + POOL_HALO), 0) + POOL_HALO
    kj = lax.broadcasted_iota(jnp.int32, (rows, rows + POOL_HALO), 1)
    back = qi - kj
    seq_pos = t * rows + lax.broadcasted_iota(jnp.int32, (rows, 1), 0)
    outs = []
    for g, win in enumerate(POOL_WINDOWS):
        cols = slice(g * POOL_GROUP, (g + 1) * POOL_GROUP)
        band = jnp.where((back >= 1) & (back < win), 1.0, 0.0).astype(BF16)
        others = (jnp.dot(band, hist_hi[:, cols], preferred_element_type=F32)
                  + jnp.dot(band, hist_lo[:, cols], preferred_element_type=F32))
        count = jnp.minimum(seq_pos + 1, win).astype(F32)
        own = u[:, cols]
        mixed = ((others + own) / count - own).astype(BF16)
        y = jnp.dot(mixed, w_ref[g], preferred_element_type=F32)
        outs.append(y * scale_ref[:, cols])
    y = jnp.concatenate(outs, axis=1)
    o_ref[0] = h + _rms(y, gout_ref[...])


def _pool_mixer(h, ln, w_group, scale, batch, seq):
    h3 = h.reshape(batch, seq, D_MODEL)
    halo_blocks = POOL_ROWS // POOL_HALO
    out = pl.pallas_call(
        _pool_kernel,
        out_shape=jax.ShapeDtypeStruct(h3.shape, F32),
        grid=(batch, seq // POOL_ROWS),
        in_specs=[
            pl.BlockSpec((1, POOL_ROWS, D_MODEL), lambda b, t: (b, t, 0)),
            pl.BlockSpec((1, POOL_HALO, D_MODEL),
                         lambda b, t: (b, jnp.maximum(t * halo_blocks - 1, 0), 0)),
            pl.BlockSpec((1, D_MODEL), lambda b, t: (0, 0)),
            pl.BlockSpec((1, D_MODEL), lambda b, t: (0, 0)),
            pl.BlockSpec((len(POOL_WINDOWS), POOL_GROUP, POOL_GROUP), lambda b, t: (0, 0, 0)),
            pl.BlockSpec((1, D_MODEL), lambda b, t: (0, 0)),
        ],
        out_specs=pl.BlockSpec((1, POOL_ROWS, D_MODEL), lambda b, t: (b, t, 0)),
        compiler_params=_params("parallel", "parallel"),
        name="pool_mixer",
    )(h3, h3, ln[0:1], ln[1:2], w_group.astype(BF16), scale.reshape(1, D_MODEL))
    return out.reshape(batch * seq, D_MODEL)


def _rope_pairs(y, cos, sin):
    outs = []
    for hb in range(y.shape[1] // (2 * LANES)):
        x1 = y[:, (2 * hb) * LANES:(2 * hb + 1) * LANES]
        x2 = y[:, (2 * hb + 1) * LANES:(2 * hb + 2) * LANES]
        outs += [x1 * cos - x2 * sin, x2 * cos + x1 * sin]
    return jnp.concatenate(outs, axis=1)


def _rope_in_lane(y, cos, sin_signed, first_half):
    half = SWA_HD // 2
    outs = []
    for c in range(y.shape[1] // LANES):
        xb = y[:, c * LANES:(c + 1) * LANES]
        partner = jnp.where(first_half, pltpu.roll(xb, LANES - half, axis=1), pltpu.roll(xb, half, axis=1))
        outs.append(xb * cos + partner * sin_signed)
    return jnp.concatenate(outs, axis=1)


def _proj_in_kernel(h_ref, g_ref, pos_ref, freq_ref, w_ref, b_ref, o_ref, xn_ref, cos_ref, sin_ref,
                    *, mode, rope_tiles, scales):
    j = pl.program_id(1)
    rows = h_ref.shape[0]

    @pl.when(j == 0)
    def _():
        _norm_rows_to(xn_ref, h_ref, g_ref[...], rows)
        ang = pos_ref[...].astype(F32) * freq_ref[...]
        cos_ref[...] = jnp.cos(ang)
        sin_ref[...] = jnp.sin(ang)

    y = jnp.dot(xn_ref[...], w_ref[...], preferred_element_type=F32) + b_ref[...]

    lane = lax.broadcasted_iota(jnp.int32, (1, LANES), 1)
    first_half = (lane % SWA_HD) < (SWA_HD // 2)

    def roped():
        cos = cos_ref[...]
        sin = sin_ref[...]
        if mode == "pairs":
            return _rope_pairs(y, cos, sin)
        return _rope_in_lane(y, cos, jnp.where(first_half, -sin, sin), first_half)

    for start, stop, factor in scales:
        @pl.when((j >= start) & (j < stop))
        def _(factor=factor):
            o_ref[...] = (roped() * factor).astype(o_ref.dtype)

    @pl.when(j >= rope_tiles)
    def _():
        o_ref[...] = y.astype(o_ref.dtype)


def _proj_in(h, g, pos, freq_row, w, bias, *, col_tile, mode, rope_tiles, scales):
    m = h.shape[0]
    n = w.shape[1]
    kern = functools.partial(_proj_in_kernel, mode=mode, rope_tiles=rope_tiles, scales=scales)
    return pl.pallas_call(
        kern,
        out_shape=jax.ShapeDtypeStruct((m, n), BF16),
        grid=(m // ROW_TILE, n // col_tile),
        in_specs=[
            pl.BlockSpec((ROW_TILE, D_MODEL), lambda i, j: (i, 0)),
            pl.BlockSpec((1, D_MODEL), lambda i, j: (0, 0)),
            pl.BlockSpec((ROW_TILE, 1), lambda i, j: (i, 0)),
            pl.BlockSpec((1, LANES), lambda i, j: (0, 0)),
            pl.BlockSpec((D_MODEL, col_tile), lambda i, j: (0, j)),
            pl.BlockSpec((1, col_tile), lambda i, j: (0, j)),
        ],
        out_specs=pl.BlockSpec((ROW_TILE, col_tile), lambda i, j: (i, j)),
        scratch_shapes=[pltpu.VMEM((ROW_TILE, D_MODEL), BF16),
                        pltpu.VMEM((ROW_TILE, LANES), F32),
                        pltpu.VMEM((ROW_TILE, LANES), F32)],
        compiler_params=_params("parallel", "arbitrary"),
        name="proj_in_" + mode,
    )(h, g, pos, freq_row, w.astype(BF16), bias.reshape(1, n))


def _proj_out_kernel(x_ref, w_ref, b_ref, g_ref, h_ref, o_ref):
    k = pl.program_id(1)
    rows = h_ref.shape[0]
    x = x_ref[...]
    for c in range(D_MODEL // OUT_COLS):
        cols = slice(c * OUT_COLS, (c + 1) * OUT_COLS)
        part = jnp.dot(x, w_ref[:, cols], preferred_element_type=F32)

        @pl.when(k == 0)
        def _():
            o_ref[:, cols] = part + b_ref[:, cols]

        @pl.when(k > 0)
        def _():
            o_ref[:, cols] += part

    @pl.when(k == pl.num_programs(1) - 1)
    def _():
        _residual_norm_rows(o_ref, h_ref, g_ref[...], rows)


def _proj_out(x, w, bias, g, h):
    m, kdim = x.shape
    return pl.pallas_call(
        _proj_out_kernel,
        out_shape=jax.ShapeDtypeStruct((m, D_MODEL), F32),
        grid=(m // ROW_TILE, kdim // PROJ_OUT_K),
        in_specs=[
            pl.BlockSpec((ROW_TILE, PROJ_OUT_K), lambda i, k: (i, k)),
            pl.BlockSpec((PROJ_OUT_K, D_MODEL), lambda i, k: (k, 0)),
            pl.BlockSpec((1, D_MODEL), lambda i, k: (0, 0)),
            pl.BlockSpec((1, D_MODEL), lambda i, k: (0, 0)),
            pl.BlockSpec((ROW_TILE, D_MODEL), lambda i, k: (i, 0)),
        ],
        out_specs=pl.BlockSpec((ROW_TILE, D_MODEL), lambda i, k: (i, 0)),
        compiler_params=_params("parallel", "arbitrary"),
        name="proj_out",
    )(x, w.astype(BF16), bias.reshape(1, D_MODEL), g, h)


def _retention_kernel(lg_ref, q_ref, k_ref, v_ref, gate_ref, gng_ref, gnb_ref, o_ref, state_ref):
    head = pl.program_id(1)
    chunk = q_ref.shape[1]

    @pl.when(pl.program_id(2) == 0)
    def _():
        state_ref[...] = jnp.zeros_like(state_ref)

    log_gamma = jnp.full((1, 1), lg_ref[head], F32)
    q = q_ref[0]
    k = k_ref[0]
    v = v_ref[0]
    qi = lax.broadcasted_iota(jnp.int32, (chunk, chunk), 0)
    kj = lax.broadcasted_iota(jnp.int32, (chunk, chunk), 1)
    rel = (qi - kj).astype(F32)
    inner_decay = jnp.where(rel >= 0, jnp.exp(jnp.maximum(rel, 0.0) * log_gamma), 0.0)
    idx = lax.broadcasted_iota(jnp.int32, (chunk, 1), 0).astype(F32)
    cross_decay = jnp.exp((idx + 1.0) * log_gamma)
    state_decay = jnp.exp((chunk - 1.0 - idx) * log_gamma)
    chunk_decay = jnp.exp(float(chunk) * log_gamma)

    scores = lax.dot_general(q, k, (((1,), (1,)), ((), ())), preferred_element_type=F32) * inner_decay
    inner = jnp.dot(scores.astype(BF16), v, preferred_element_type=F32)
    state = state_ref[...]
    cross = jnp.dot(q, state.astype(BF16), preferred_element_type=F32) * cross_decay
    k_decayed = (k.astype(F32) * state_decay).astype(BF16)
    state_ref[...] = state * chunk_decay + lax.dot_general(
        k_decayed, v, (((0,), (0,)), ((), ())), preferred_element_type=F32)

    o = inner + cross
    mu = jnp.mean(o, axis=-1, keepdims=True)
    var = jnp.mean(jnp.square(o - mu), axis=-1, keepdims=True)
    o = (o - mu) * lax.rsqrt(var + EPS)
    o = o * gng_ref[...] + gnb_ref[...]
    o_ref[0] = (jax.nn.silu(gate_ref[0].astype(F32)) * o).astype(o_ref.dtype)


def _retention_core(hproj, gn_g, gn_b, batch, seq):
    hp = hproj.reshape(batch, seq, 6 * D_MODEL)
    log_gamma = jnp.log(1.0 - 2.0 ** (-5.0 - jnp.arange(RET_HEADS, dtype=F32)))
    k_blk0 = D_MODEL // RET_DK
    v_blk0 = 2 * D_MODEL // RET_DV
    g_blk0 = 4 * D_MODEL // RET_DV
    out = pl.pallas_call(
        _retention_kernel,
        out_shape=jax.ShapeDtypeStruct((batch, seq, 2 * D_MODEL), BF16),
        grid=(batch, RET_HEADS, seq // RET_CHUNK),
        in_specs=[
            pl.BlockSpec(memory_space=pltpu.SMEM),
            pl.BlockSpec((1, RET_CHUNK, RET_DK), lambda b, hd, c: (b, c, hd)),
            pl.BlockSpec((1, RET_CHUNK, RET_DK), lambda b, hd, c: (b, c, k_blk0 + hd)),
            pl.BlockSpec((1, RET_CHUNK, RET_DV), lambda b, hd, c: (b, c, v_blk0 + hd)),
            pl.BlockSpec((1, RET_CHUNK, RET_DV), lambda b, hd, c: (b, c, g_blk0 + hd)),
            pl.BlockSpec((1, RET_DV), lambda b, hd, c: (0, hd)),
            pl.BlockSpec((1, RET_DV), lambda b, hd, c: (0, hd)),
        ],
        out_specs=pl.BlockSpec((1, RET_CHUNK, RET_DV), lambda b, hd, c: (b, c, hd)),
        scratch_shapes=[pltpu.VMEM((RET_DK, RET_DV), F32)],
        compiler_params=_params("parallel", "parallel", "arbitrary"),
        name="retention_core",
    )(log_gamma, hp, hp, hp, hp, gn_g.reshape(1, 2 * D_MODEL), gn_b.reshape(1, 2 * D_MODEL))
    return out.reshape(batch * seq, 2 * D_MODEL)


def _swa_kernel(sink_ref, q_ref, kc_ref, vc_ref, kp_ref, vp_ref, o_ref):
    n = pl.program_id(1)
    blk = SWA_BLOCK
    lane = lax.broadcasted_iota(jnp.int32, (1, LANES), 1)
    low = lane < SWA_HD

    def both_halves(t, odd):
        t = t.astype(F32)
        keep = jnp.logical_not(low) if odd else low
        return jnp.where(keep, t, pltpu.roll(t, SWA_HD, axis=1)).astype(BF16)

    k_all = jnp.concatenate([kp_ref[0], kc_ref[0]], axis=0)
    v_all = jnp.concatenate([vp_ref[0], vc_ref[0]], axis=0)
    qi = lax.broadcasted_iota(jnp.int32, (blk, 2 * blk), 0)
    kj = lax.broadcasted_iota(jnp.int32, (blk, 2 * blk), 1)
    diff = qi + blk - kj
    first_key = jnp.where(n > 0, 0, blk)
    allowed = (diff >= 0) & (diff < SWA_WINDOW) & (kj >= first_key)
    allowed = jnp.concatenate([allowed] * SWA_GROUP, axis=0)

    for kv in range(SWA_KV_HEADS):
        kv_cols = slice((kv // 2) * LANES, (kv // 2 + 1) * LANES)
        k2 = both_halves(k_all[:, kv_cols], kv % 2)
        v2 = both_halves(v_all[:, kv_cols], kv % 2)
        q_rows = []
        sink_rows = []
        for jh in range(SWA_GROUP):
            head = kv * SWA_GROUP + jh
            qp = q_ref[0, :, (head // 2) * LANES:(head // 2 + 1) * LANES].astype(F32)
            keep = jnp.logical_not(low) if head % 2 else low
            q_rows.append(jnp.where(keep, qp, 0.0).astype(BF16))
            sink_rows.append(jnp.full((blk, 1), sink_ref[head], F32))
        q_stack = jnp.concatenate(q_rows, axis=0)
        sink = jnp.concatenate(sink_rows, axis=0)
        s = lax.dot_general(q_stack, k2, (((1,), (1,)), ((), ())), preferred_element_type=F32)
        s = jnp.where(allowed, s, -jnp.inf)
        m = jnp.maximum(jnp.max(s, axis=-1, keepdims=True), sink)
        e = jnp.exp(s - m)
        denom = jnp.sum(e, axis=-1, keepdims=True) + jnp.exp(sink - m)
        p = (e / denom).astype(BF16)
        ov = jnp.dot(p, v2, preferred_element_type=F32)
        for jp in range(SWA_GROUP // 2):
            even = ov[(2 * jp) * blk:(2 * jp + 1) * blk]
            odd = ov[(2 * jp + 1) * blk:(2 * jp + 2) * blk]
            c = kv * (SWA_GROUP // 2) + jp
            o_ref[0, :, c * LANES:(c + 1) * LANES] = jnp.where(low, even, odd).astype(o_ref.dtype)


def _swa_core(hproj, sinks, batch, seq):
    width = hproj.shape[1]
    hp = hproj.reshape(batch, seq, width)
    q_width = SWA_HEADS * SWA_HD
    kv_width = SWA_KV_HEADS * SWA_HD
    k_blk = q_width // kv_width
    v_blk = k_blk + 1
    prev = lambda n: jnp.maximum(n - 1, 0)
    out = pl.pallas_call(
        _swa_kernel,
        out_shape=jax.ShapeDtypeStruct((batch, seq, q_width), BF16),
        grid=(batch, seq // SWA_BLOCK),
        in_specs=[
            pl.BlockSpec(memory_space=pltpu.SMEM),
            pl.BlockSpec((1, SWA_BLOCK, q_width), lambda b, n: (b, n, 0)),
            pl.BlockSpec((1, SWA_BLOCK, kv_width), lambda b, n: (b, n, k_blk)),
            pl.BlockSpec((1, SWA_BLOCK, kv_width), lambda b, n: (b, n, v_blk)),
            pl.BlockSpec((1, SWA_BLOCK, kv_width), lambda b, n: (b, prev(n), k_blk)),
            pl.BlockSpec((1, SWA_BLOCK, kv_width), lambda b, n: (b, prev(n), v_blk)),
        ],
        out_specs=pl.BlockSpec((1, SWA_BLOCK, q_width), lambda b, n: (b, n, 0)),
        compiler_params=_params("parallel", "parallel"),
        name="swa_core",
    )(sinks, hp, hp, hp, hp, hp)
    return out.reshape(batch * seq, q_width)


def _retention_mixer(h, ln, pos, w_in, gn_g, gn_b, w_out, batch, seq):
    inv_freq = ROPE_THETA ** (-jnp.linspace(0.0, 1.0, RET_DK // 2, dtype=F32))
    q_tiles = D_MODEL // RET_COL_TILE
    hproj = _proj_in(
        h, ln[0:1], pos, inv_freq.reshape(1, LANES), w_in, jnp.zeros((w_in.shape[1],), F32),
        col_tile=RET_COL_TILE, mode="pairs", rope_tiles=2 * q_tiles,
        scales=((0, q_tiles, 1.0), (q_tiles, 2 * q_tiles, RET_DK ** -0.5)))
    gated = _retention_core(hproj, gn_g, gn_b, batch, seq)
    return _proj_out(gated, w_out, jnp.zeros((D_MODEL,), F32), ln[1:2], h)


def _swa_mixer(h, ln, pos, w_in, b_in, sinks, w_out, b_out, batch, seq):
    inv_freq = ROPE_THETA ** (-jnp.arange(0, SWA_HD, 2, dtype=F32) / SWA_HD)
    freq_row = jnp.tile(inv_freq, LANES // (SWA_HD // 2)).reshape(1, LANES)
    col_tile = SWA_KV_HEADS * SWA_HD
    q_tiles = SWA_HEADS * SWA_HD // col_tile
    hproj = _proj_in(
        h, ln[0:1], pos, freq_row, w_in, b_in,
        col_tile=col_tile, mode="lane", rope_tiles=q_tiles + 1,
        scales=((0, q_tiles, SWA_HD ** -0.5), (q_tiles, q_tiles + 1, 1.0)))
    attn = _swa_core(hproj, sinks, batch, seq)
    return _proj_out(attn, w_out, b_out, ln[1:2], h)


def kernel(x, positions, ln_ffn1, ln_mix, ln_ffn2, ffn_w_in, ffn_w_out, pool_w, pool_scale, ret_w_in,
           ret_gn_g, ret_gn_b, ret_w_out, swa_w_in, swa_b_in, swa_sinks, swa_w_out, swa_b_out):
    batch, seq, _ = x.shape
    h = x.reshape(batch * seq, D_MODEL)
    pos = positions.reshape(batch * seq, 1)
    w_in_all = ffn_w_in.astype(BF16)
    w_out_all = ffn_w_out.astype(BF16)
    for i in range(DEPTH):
        h = _ffn(h, ln_ffn1[i], w_in_all, w_out_all, i, 0)
        kind, j = i % N_MIXERS, i // N_MIXERS
        if kind == 0:
            h = _pool_mixer(h, ln_mix[i], pool_w[j], pool_scale[j], batch, seq)
        elif kind == 1:
            h = _retention_mixer(h, ln_mix[i], pos, ret_w_in[j], ret_gn_g[j], ret_gn_b[j], ret_w_out[j],
                                 batch, seq)
        else:
            h = _swa_mixer(h, ln_mix[i], pos, swa_w_in[j], swa_b_in[j], swa_sinks[j], swa_w_out[j],
                           swa_b_out[j], batch, seq)
        h = _ffn(h, ln_ffn2[i], w_in_all, w_out_all, i, 1)
    return h.reshape(batch, seq, D_MODEL)
```

```python
import functools

import jax
import jax.numpy as jnp
from jax import lax
from jax.experimental import pallas as pl
from jax.experimental.pallas import tpu as pltpu

F32 = jnp.float32
BF16 = jnp.bfloat16

D_MODEL = 2048
DEPTH = 4
N_MIXERS = 3
EPS = 1e-6
D_FF = 5504
POOL_WINDOWS = (2, 4, 8, 16)
POOL_GROUP = D_MODEL // len(POOL_WINDOWS)
POOL_HALO = 16
RET_HEADS = 8
RET_DK = D_MODEL // RET_HEADS
RET_DV = 2 * D_MODEL // RET_HEADS
SWA_HEADS = 32
SWA_KV_HEADS = 4
SWA_HD = 64
SWA_GROUP = SWA_HEADS // SWA_KV_HEADS
SWA_WINDOW = 128
SWA_BLOCK = 128
ROPE_THETA = 10000.0

LANES = 128
VMEM_LIMIT_BYTES = 58 * 1024 * 1024

ROW_TILE = 1024
NORM_ROWS = 128
FF_TILE = 256
PROJ_OUT_K = 1024
POOL_ROWS = 256
RET_COL_TILE = 1024
RET_CHUNK = 256


def _rms(x, g):
    return x * lax.rsqrt(jnp.mean(x * x, axis=-1, keepdims=True) + EPS) * g


def _params(*sem):
    return pltpu.CompilerParams(dimension_semantics=sem, vmem_limit_bytes=VMEM_LIMIT_BYTES)


def _norm_rows_to(dst_ref, src_ref, g, rows, zero_ref=None):
    def body(i, carry):
        r = pl.ds(pl.multiple_of(i * NORM_ROWS, NORM_ROWS), NORM_ROWS)
        dst_ref[r, :] = _rms(src_ref[r, :], g).astype(dst_ref.dtype)
        if zero_ref is not None:
            zero_ref[r, :] = jnp.zeros((NORM_ROWS, zero_ref.shape[1]), zero_ref.dtype)
        return carry
    lax.fori_loop(0, rows // NORM_ROWS, body, 0)


def _residual_norm_rows(o_ref, h_ref, g, rows):
    def body(i, carry):
        r = pl.ds(pl.multiple_of(i * NORM_ROWS, NORM_ROWS), NORM_ROWS)
        o_ref[r, :] = h_ref[r, :] + _rms(o_ref[r, :], g)
        return carry
    lax.fori_loop(0, rows // NORM_ROWS, body, 0)


def _ff_offset(f, base=0):
    return LANES * (base // LANES + jnp.minimum(f * (FF_TILE // LANES), (D_FF - FF_TILE) // LANES))


def _ffn_kernel(h_ref, g1_ref, g2_ref, wg_ref, wu_ref, wo_ref, o_ref, xn_ref):
    f = pl.program_id(1)
    rows = h_ref.shape[0]

    @pl.when(f == 0)
    def _():
        _norm_rows_to(xn_ref, h_ref, g1_ref[...], rows, zero_ref=o_ref)

    xn = xn_ref[...]
    gate = jnp.dot(xn, wg_ref[...].astype(BF16), preferred_element_type=F32)
    up = jnp.dot(xn, wu_ref[...].astype(BF16), preferred_element_type=F32)
    repeated = f * FF_TILE - _ff_offset(f)
    fresh = lax.broadcasted_iota(jnp.int32, (1, FF_TILE), 1) >= repeated
    hid = jnp.where(fresh, jax.nn.silu(gate) * up, 0.0).astype(BF16)
    o_ref[...] += jnp.dot(hid, wo_ref[...].astype(BF16), preferred_element_type=F32)

    @pl.when(f == pl.num_programs(1) - 1)
    def _():
        _residual_norm_rows(o_ref, h_ref, 0.5 * g2_ref[...], rows)


def _ffn(h, ln, w_in_all, w_out_all, layer, half):
    m = h.shape[0]
    elem = pl.Element
    return pl.pallas_call(
        _ffn_kernel,
        out_shape=jax.ShapeDtypeStruct((m, D_MODEL), F32),
        grid=(m // ROW_TILE, pl.cdiv(D_FF, FF_TILE)),
        in_specs=[
            pl.BlockSpec((ROW_TILE, D_MODEL), lambda i, f: (i, 0)),
            pl.BlockSpec((1, D_MODEL), lambda i, f: (0, 0)),
            pl.BlockSpec((1, D_MODEL), lambda i, f: (0, 0)),
            pl.BlockSpec((None, None, elem(D_MODEL), elem(FF_TILE)),
                         lambda i, f: (layer, half, 0, _ff_offset(f))),
            pl.BlockSpec((None, None, elem(D_MODEL), elem(FF_TILE)),
                         lambda i, f: (layer, half, 0, _ff_offset(f, D_FF))),
            pl.BlockSpec((None, None, elem(FF_TILE), elem(D_MODEL)),
                         lambda i, f: (layer, half, _ff_offset(f), 0)),
        ],
        out_specs=pl.BlockSpec((ROW_TILE, D_MODEL), lambda i, f: (i, 0)),
        scratch_shapes=[pltpu.VMEM((ROW_TILE, D_MODEL), BF16)],
        compiler_params=_params("parallel", "arbitrary"),
        name="ffn",
    )(h, ln[0:1], ln[1:2], w_in_all, w_in_all, w_out_all)


def _pool_kernel(h_ref, halo_ref, gin_ref, gout_ref, w_ref, scale_ref, o_ref):
    t = pl.program_id(1)
    rows = h_ref.shape[1]
    h = h_ref[0]
    gin = gin_ref[...]
    u = _rms(h, gin)
    has_prev = (t > 0).astype(F32)
    halo = _rms(halo_ref[0], gin) * has_prev
    hist = jnp.concatenate([halo, u], axis=0)
    hist_hi = hist.astype(BF16)
    hist_lo = (hist - hist_hi.astype(F32)).astype(BF16)
    qi = lax.broadcasted_iota(jnp.int32, (rows, rows + POOL_HALO), 0) + POOL_HALO
    kj = lax.broadcasted_iota(jnp.int32, (rows, rows + POOL_HALO), 1)
    back = qi - kj
    seq_pos = t * rows + lax.broadcasted_iota(jnp.int32, (rows, 1), 0)
    outs = []
    for g, win in enumerate(POOL_WINDOWS):
        cols = slice(g * POOL_GROUP, (g + 1) * POOL_GROUP)
        band = jnp.where((back >= 1) & (back < win), 1.0, 0.0).astype(BF16)
        others = (jnp.dot(band, hist_hi[:, cols], preferred_element_type=F32)
                  + jnp.dot(band, hist_lo[:, cols], preferred_element_type=F32))
        count = jnp.minimum(seq_pos + 1, win).astype(F32)
        own = u[:, cols]
        mixed = ((others + own) / count - own).astype(BF16)
        y = jnp.dot(mixed, w_ref[g], preferred_element_type=F32)
        outs.append(y * scale_ref[:, cols])
    y = jnp.concatenate(outs, axis=1)
    o_ref[0] = h + _rms(y, gout_ref[...])


def _pool_mixer(h, ln, w_group, scale, batch, seq):
    h3 = h.reshape(batch, seq, D_MODEL)
    halo_blocks = POOL_ROWS // POOL_HALO
    out = pl.pallas_call(
        _pool_kernel,
        out_shape=jax.ShapeDtypeStruct(h3.shape, F32),
        grid=(batch, seq // POOL_ROWS),
        in_specs=[
            pl.BlockSpec((1, POOL_ROWS, D_MODEL), lambda b, t: (b, t, 0)),
            pl.BlockSpec((1, POOL_HALO, D_MODEL),
                         lambda b, t: (b, jnp.maximum(t * halo_blocks - 1, 0), 0)),
            pl.BlockSpec((1, D_MODEL), lambda b, t: (0, 0)),
            pl.BlockSpec((1, D_MODEL), lambda b, t: (0, 0)),
            pl.BlockSpec((len(POOL_WINDOWS), POOL_GROUP, POOL_GROUP), lambda b, t: (0, 0, 0)),
            pl.BlockSpec((1, D_MODEL), lambda b, t: (0, 0)),
        ],
        out_specs=pl.BlockSpec((1, POOL_ROWS, D_MODEL), lambda b, t: (b, t, 0)),
        compiler_params=_params("parallel", "parallel"),
        name="pool_mixer",
    )(h3, h3, ln[0:1], ln[1:2], w_group.astype(BF16), scale.reshape(1, D_MODEL))
    return out.reshape(batch * seq, D_MODEL)


def _rope_pairs(y, cos, sin):
    outs = []
    for hb in range(y.shape[1] // (2 * LANES)):
        x1 = y[:, (2 * hb) * LANES:(2 * hb + 1) * LANES]
        x2 = y[:, (2 * hb + 1) * LANES:(2 * hb + 2) * LANES]
        outs += [x1 * cos - x2 * sin, x2 * cos + x1 * sin]
    return jnp.concatenate(outs, axis=1)


def _rope_in_lane(y, cos, sin_signed, first_half):
    half = SWA_HD // 2
    outs = []
    for c in range(y.shape[1] // LANES):
        xb = y[:, c * LANES:(c + 1) * LANES]
        partner = jnp.where(first_half, pltpu.roll(xb, LANES - half, axis=1), pltpu.roll(xb, half, axis=1))
        outs.append(xb * cos + partner * sin_signed)
    return jnp.concatenate(outs, axis=1)


def _proj_in_kernel(h_ref, g_ref, pos_ref, freq_ref, w_ref, b_ref, o_ref, xn_ref, cos_ref, sin_ref,
                    *, mode, rope_tiles, scales):
    j = pl.program_id(1)
    rows = h_ref.shape[0]

    @pl.when(j == 0)
    def _():
        _norm_rows_to(xn_ref, h_ref, g_ref[...], rows)
        ang = pos_ref[...].astype(F32) * freq_ref[...]
        cos_ref[...] = jnp.cos(ang)
        sin_ref[...] = jnp.sin(ang)

    y = jnp.dot(xn_ref[...], w_ref[...].astype(BF16), preferred_element_type=F32) + b_ref[...]

    lane = lax.broadcasted_iota(jnp.int32, (1, LANES), 1)
    first_half = (lane % SWA_HD) < (SWA_HD // 2)

    def roped():
        cos = cos_ref[...]
        sin = sin_ref[...]
        if mode == "pairs":
            return _rope_pairs(y, cos, sin)
        return _rope_in_lane(y, cos, jnp.where(first_half, -sin, sin), first_half)

    for start, stop, factor in scales:
        @pl.when((j >= start) & (j < stop))
        def _(factor=factor):
            o_ref[...] = (roped() * factor).astype(o_ref.dtype)

    @pl.when(j >= rope_tiles)
    def _():
        o_ref[...] = y.astype(o_ref.dtype)


def _proj_in(h, g, pos, freq_row, w, bias, *, col_tile, mode, rope_tiles, scales):
    m = h.shape[0]
    n = w.shape[1]
    kern = functools.partial(_proj_in_kernel, mode=mode, rope_tiles=rope_tiles, scales=scales)
    return pl.pallas_call(
        kern,
        out_shape=jax.ShapeDtypeStruct((m, n), BF16),
        grid=(m // ROW_TILE, n // col_tile),
        in_specs=[
            pl.BlockSpec((ROW_TILE, D_MODEL), lambda i, j: (i, 0)),
            pl.BlockSpec((1, D_MODEL), lambda i, j: (0, 0)),
            pl.BlockSpec((ROW_TILE, 1), lambda i, j: (i, 0)),
            pl.BlockSpec((1, LANES), lambda i, j: (0, 0)),
            pl.BlockSpec((D_MODEL, col_tile), lambda i, j: (0, j)),
            pl.BlockSpec((1, col_tile), lambda i, j: (0, j)),
        ],
        out_specs=pl.BlockSpec((ROW_TILE, col_tile), lambda i, j: (i, j)),
        scratch_shapes=[pltpu.VMEM((ROW_TILE, D_MODEL), BF16),
                        pltpu.VMEM((ROW_TILE, LANES), F32),
                        pltpu.VMEM((ROW_TILE, LANES), F32)],
        compiler_params=_params("parallel", "arbitrary"),
        name="proj_in_" + mode,
    )(h, g, pos, freq_row, w, bias.reshape(1, n))


def _proj_out_kernel(x_ref, w_ref, b_ref, g_ref, h_ref, o_ref):
    k = pl.program_id(1)
    rows = h_ref.shape[0]

    def partial_product():
        return jnp.dot(x_ref[...], w_ref[...].astype(BF16), preferred_element_type=F32)

    @pl.when(k == 0)
    def _():
        o_ref[...] = partial_product() + b_ref[...]

    @pl.when(k > 0)
    def _():
        o_ref[...] += partial_product()

    @pl.when(k == pl.num_programs(1) - 1)
    def _():
        _residual_norm_rows(o_ref, h_ref, g_ref[...], rows)


def _proj_out(x, w, bias, g, h):
    m, kdim = x.shape
    return pl.pallas_call(
        _proj_out_kernel,
        out_shape=jax.ShapeDtypeStruct((m, D_MODEL), F32),
        grid=(m // ROW_TILE, kdim // PROJ_OUT_K),
        in_specs=[
            pl.BlockSpec((ROW_TILE, PROJ_OUT_K), lambda i, k: (i, k)),
            pl.BlockSpec((PROJ_OUT_K, D_MODEL), lambda i, k: (k, 0)),
            pl.BlockSpec((1, D_MODEL), lambda i, k: (0, 0)),
            pl.BlockSpec((1, D_MODEL), lambda i, k: (0, 0)),
            pl.BlockSpec((ROW_TILE, D_MODEL), lambda i, k: (i, 0)),
        ],
        out_specs=pl.BlockSpec((ROW_TILE, D_MODEL), lambda i, k: (i, 0)),
        compiler_params=_params("parallel", "arbitrary"),
        name="proj_out",
    )(x, w, bias.reshape(1, D_MODEL), g, h)


def _retention_kernel(lg_ref, q_ref, k_ref, v_ref, gate_ref, gng_ref, gnb_ref, o_ref, state_ref):
    head = pl.program_id(1)
    chunk = q_ref.shape[1]

    @pl.when(pl.program_id(2) == 0)
    def _():
        state_ref[...] = jnp.zeros_like(state_ref)

    log_gamma = jnp.full((1, 1), lg_ref[head], F32)
    q = q_ref[0]
    k = k_ref[0]
    v = v_ref[0]
    qi = lax.broadcasted_iota(jnp.int32, (chunk, chunk), 0)
    kj = lax.broadcasted_iota(jnp.int32, (chunk, chunk), 1)
    rel = (qi - kj).astype(F32)
    inner_decay = jnp.where(rel >= 0, jnp.exp(jnp.maximum(rel, 0.0) * log_gamma), 0.0)
    idx = lax.broadcasted_iota(jnp.int32, (chunk, 1), 0).astype(F32)
    cross_decay = jnp.exp((idx + 1.0) * log_gamma)
    state_decay = jnp.exp((chunk - 1.0 - idx) * log_gamma)
    chunk_decay = jnp.exp(float(chunk) * log_gamma)

    scores = lax.dot_general(q, k, (((1,), (1,)), ((), ())), preferred_element_type=F32) * inner_decay
    inner = jnp.dot(scores.astype(BF16), v, preferred_element_type=F32)
    state = state_ref[...]
    cross = jnp.dot(q, state.astype(BF16), preferred_element_type=F32) * cross_decay
    k_decayed = (k.astype(F32) * state_decay).astype(BF16)
    state_ref[...] = state * chunk_decay + lax.dot_general(
        k_decayed, v, (((0,), (0,)), ((), ())), preferred_element_type=F32)

    o = inner + cross
    mu = jnp.mean(o, axis=-1, keepdims=True)
    var = jnp.mean(jnp.square(o - mu), axis=-1, keepdims=True)
    o = (o - mu) * lax.rsqrt(var + EPS)
    o = o * gng_ref[...] + gnb_ref[...]
    o_ref[0] = (jax.nn.silu(gate_ref[0].astype(F32)) * o).astype(o_ref.dtype)


def _retention_core(hproj, gn_g, gn_b, batch, seq):
    hp = hproj.reshape(batch, seq, 6 * D_MODEL)
    log_gamma = jnp.log(1.0 - 2.0 ** (-5.0 - jnp.arange(RET_HEADS, dtype=F32)))
    k_blk0 = D_MODEL // RET_DK
    v_blk0 = 2 * D_MODEL // RET_DV
    g_blk0 = 4 * D_MODEL // RET_DV
    out = pl.pallas_call(
        _retention_kernel,
        out_shape=jax.ShapeDtypeStruct((batch, seq, 2 * D_MODEL), BF16),
        grid=(batch, RET_HEADS, seq // RET_CHUNK),
        in_specs=[
            pl.BlockSpec(memory_space=pltpu.SMEM),
            pl.BlockSpec((1, RET_CHUNK, RET_DK), lambda b, hd, c: (b, c, hd)),
            pl.BlockSpec((1, RET_CHUNK, RET_DK), lambda b, hd, c: (b, c, k_blk0 + hd)),
            pl.BlockSpec((1, RET_CHUNK, RET_DV), lambda b, hd, c: (b, c, v_blk0 + hd)),
            pl.BlockSpec((1, RET_CHUNK, RET_DV), lambda b, hd, c: (b, c, g_blk0 + hd)),
            pl.BlockSpec((1, RET_DV), lambda b, hd, c: (0, hd)),
            pl.BlockSpec((1, RET_DV), lambda b, hd, c: (0, hd)),
        ],
        out_specs=pl.BlockSpec((1, RET_CHUNK, RET_DV), lambda b, hd, c: (b, c, hd)),
        scratch_shapes=[pltpu.VMEM((RET_DK, RET_DV), F32)],
        compiler_params=_params("parallel", "parallel", "arbitrary"),
        name="retention_core",
    )(log_gamma, hp, hp, hp, hp, gn_g.reshape(1, 2 * D_MODEL), gn_b.reshape(1, 2 * D_MODEL))
    return out.reshape(batch * seq, 2 * D_MODEL)


def _swa_kernel(sink_ref, q_ref, kc_ref, vc_ref, kp_ref, vp_ref, o_ref):
    n = pl.program_id(1)
    blk = SWA_BLOCK
    lane = lax.broadcasted_iota(jnp.int32, (1, LANES), 1)
    low = lane < SWA_HD

    def both_halves(t, odd):
        t = t.astype(F32)
        keep = jnp.logical_not(low) if odd else low
        return jnp.where(keep, t, pltpu.roll(t, SWA_HD, axis=1)).astype(BF16)

    k_all = jnp.concatenate([kp_ref[0], kc_ref[0]], axis=0)
    v_all = jnp.concatenate([vp_ref[0], vc_ref[0]], axis=0)
    qi = lax.broadcasted_iota(jnp.int32, (blk, 2 * blk), 0)
    kj = lax.broadcasted_iota(jnp.int32, (blk, 2 * blk), 1)
    diff = qi + blk - kj
    first_key = jnp.where(n > 0, 0, blk)
    allowed = (diff >= 0) & (diff < SWA_WINDOW) & (kj >= first_key)

    for kv in range(SWA_KV_HEADS):
        kv_cols = slice((kv // 2) * LANES, (kv // 2 + 1) * LANES)
        k2 = both_halves(k_all[:, kv_cols], kv % 2)
        v2 = both_halves(v_all[:, kv_cols], kv % 2)
        outs = []
        for jh in range(SWA_GROUP):
            head = kv * SWA_GROUP + jh
            qp = q_ref[0, :, (head // 2) * LANES:(head // 2 + 1) * LANES].astype(F32)
            keep = jnp.logical_not(low) if head % 2 else low
            qh = jnp.where(keep, qp, 0.0).astype(BF16)
            sink = jnp.full((blk, 1), sink_ref[head], F32)
            s = lax.dot_general(qh, k2, (((1,), (1,)), ((), ())), preferred_element_type=F32)
            s = jnp.where(allowed, s, -jnp.inf)
            m = jnp.maximum(jnp.max(s, axis=-1, keepdims=True), sink)
            e = jnp.exp(s - m)
            denom = jnp.sum(e, axis=-1, keepdims=True) + jnp.exp(sink - m)
            ov = jnp.dot(e.astype(BF16), v2, preferred_element_type=F32)
            outs.append(ov / denom)
        for jp in range(SWA_GROUP // 2):
            c = kv * (SWA_GROUP // 2) + jp
            o_ref[0, :, c * LANES:(c + 1) * LANES] = jnp.where(low, outs[2 * jp], outs[2 * jp + 1]).astype(
                o_ref.dtype)


def _swa_core(hproj, sinks, batch, seq):
    width = hproj.shape[1]
    hp = hproj.reshape(batch, seq, width)
    q_width = SWA_HEADS * SWA_HD
    kv_width = SWA_KV_HEADS * SWA_HD
    k_blk = q_width // kv_width
    v_blk = k_blk + 1
    prev = lambda n: jnp.maximum(n - 1, 0)
    out = pl.pallas_call(
        _swa_kernel,
        out_shape=jax.ShapeDtypeStruct((batch, seq, q_width), BF16),
        grid=(batch, seq // SWA_BLOCK),
        in_specs=[
            pl.BlockSpec(memory_space=pltpu.SMEM),
            pl.BlockSpec((1, SWA_BLOCK, q_width), lambda b, n: (b, n, 0)),
            pl.BlockSpec((1, SWA_BLOCK, kv_width), lambda b, n: (b, n, k_blk)),
            pl.BlockSpec((1, SWA_BLOCK, kv_width), lambda b, n: (b, n, v_blk)),
            pl.BlockSpec((1, SWA_BLOCK, kv_width), lambda b, n: (b, prev(n), k_blk)),
            pl.BlockSpec((1, SWA_BLOCK, kv_width), lambda b, n: (b, prev(n), v_blk)),
        ],
        out_specs=pl.BlockSpec((1, SWA_BLOCK, q_width), lambda b, n: (b, n, 0)),
        compiler_params=_params("parallel", "parallel"),
        name="swa_core",
    )(sinks, hp, hp, hp, hp, hp)
    return out.reshape(batch * seq, q_width)


def _retention_mixer(h, ln, pos, w_in, gn_g, gn_b, w_out, batch, seq):
    inv_freq = ROPE_THETA ** (-jnp.linspace(0.0, 1.0, RET_DK // 2, dtype=F32))
    q_tiles = D_MODEL // RET_COL_TILE
    hproj = _proj_in(
        h, ln[0:1], pos, inv_freq.reshape(1, LANES), w_in, jnp.zeros((w_in.shape[1],), F32),
        col_tile=RET_COL_TILE, mode="pairs", rope_tiles=2 * q_tiles,
        scales=((0, q_tiles, 1.0), (q_tiles, 2 * q_tiles, RET_DK ** -0.5)))
    gated = _retention_core(hproj, gn_g, gn_b, batch, seq)
    return _proj_out(gated, w_out, jnp.zeros((D_MODEL,), F32), ln[1:2], h)


def _swa_mixer(h, ln, pos, w_in, b_in, sinks, w_out, b_out, batch, seq):
    inv_freq = ROPE_THETA ** (-jnp.arange(0, SWA_HD, 2, dtype=F32) / SWA_HD)
    freq_row = jnp.tile(inv_freq, LANES // (SWA_HD // 2)).reshape(1, LANES)
    col_tile = SWA_KV_HEADS * SWA_HD
    q_tiles = SWA_HEADS * SWA_HD // col_tile
    hproj = _proj_in(
        h, ln[0:1], pos, freq_row, w_in, b_in,
        col_tile=col_tile, mode="lane", rope_tiles=q_tiles + 1,
        scales=((0, q_tiles, SWA_HD ** -0.5), (q_tiles, q_tiles + 1, 1.0)))
    attn = _swa_core(hproj, sinks, batch, seq)
    return _proj_out(attn, w_out, b_out, ln[1:2], h)


def kernel(x, positions, ln_ffn1, ln_mix, ln_ffn2, ffn_w_in, ffn_w_out, pool_w, pool_scale, ret_w_in,
           ret_gn_g, ret_gn_b, ret_w_out, swa_w_in, swa_b_in, swa_sinks, swa_w_out, swa_b_out):
    batch, seq, _ = x.shape
    h = x.reshape(batch * seq, D_MODEL)
    pos = positions.reshape(batch * seq, 1)
    for i in range(DEPTH):
        h = _ffn(h, ln_ffn1[i], ffn_w_in, ffn_w_out, i, 0)
        kind, j = i % N_MIXERS, i // N_MIXERS
        if kind == 0:
            h = _pool_mixer(h, ln_mix[i], pool_w[j], pool_scale[j], batch, seq)
        elif kind == 1:
            h = _retention_mixer(h, ln_mix[i], pos, ret_w_in[j], ret_gn_g[j], ret_gn_b[j], ret_w_out[j],
                                 batch, seq)
        else:
            h = _swa_mixer(h, ln_mix[i], pos, swa_w_in[j], swa_b_in[j], swa_sinks[j], swa_w_out[j],
                           swa_b_out[j], batch, seq)
        h = _ffn(h, ln_ffn2[i], ffn_w_in, ffn_w_out, i, 1)
    return h.reshape(batch, seq, D_MODEL)
```

```python
import functools

import jax
import jax.numpy as jnp
from jax import lax
from jax.experimental import pallas as pl
from jax.experimental.pallas import tpu as pltpu

F32 = jnp.float32
BF16 = jnp.bfloat16

D_MODEL = 2048
DEPTH = 4
N_MIXERS = 3
EPS = 1e-6
D_FF = 5504
POOL_WINDOWS = (2, 4, 8, 16)
POOL_GROUP = D_MODEL // len(POOL_WINDOWS)
POOL_HALO = 16
RET_HEADS = 8
RET_DK = D_MODEL // RET_HEADS
RET_DV = 2 * D_MODEL // RET_HEADS
SWA_HEADS = 32
SWA_KV_HEADS = 4
SWA_HD = 64
SWA_GROUP = SWA_HEADS // SWA_KV_HEADS
SWA_WINDOW = 128
SWA_BLOCK = 128
ROPE_THETA = 10000.0

LANES = 128
VMEM_LIMIT_BYTES = 58 * 1024 * 1024

ROW_TILE = 1024
NORM_ROWS = 128
FF_TILE = 256
PROJ_OUT_K = 1024
POOL_ROWS = 256
RET_COL_TILE = 1024
SWA_COL_TILE = 512
RET_HEADS_PER_STEP = 8
RET_CHUNK = 256


def _rms(x, g):
    return x * lax.rsqrt(jnp.mean(x * x, axis=-1, keepdims=True) + EPS) * g


def _params(*sem):
    return pltpu.CompilerParams(dimension_semantics=sem, vmem_limit_bytes=VMEM_LIMIT_BYTES)


def _norm_rows_to(dst_ref, src_ref, g, rows, zero_ref=None):
    def body(i, carry):
        r = pl.ds(pl.multiple_of(i * NORM_ROWS, NORM_ROWS), NORM_ROWS)
        dst_ref[r, :] = _rms(src_ref[r, :], g).astype(dst_ref.dtype)
        if zero_ref is not None:
            zero_ref[r, :] = jnp.zeros((NORM_ROWS, zero_ref.shape[1]), zero_ref.dtype)
        return carry
    lax.fori_loop(0, rows // NORM_ROWS, body, 0)


def _residual_norm_rows(o_ref, h_ref, g, rows):
    def body(i, carry):
        r = pl.ds(pl.multiple_of(i * NORM_ROWS, NORM_ROWS), NORM_ROWS)
        o_ref[r, :] = h_ref[r, :] + _rms(o_ref[r, :], g)
        return carry
    lax.fori_loop(0, rows // NORM_ROWS, body, 0)


def _ff_offset(f, base=0):
    return LANES * (base // LANES + jnp.minimum(f * (FF_TILE // LANES), (D_FF - FF_TILE) // LANES))


def _ffn_kernel(h_ref, g1_ref, g2_ref, wg_ref, wu_ref, wo_ref, o_ref, xn_ref):
    f = pl.program_id(1)
    rows = h_ref.shape[0]

    @pl.when(f == 0)
    def _():
        _norm_rows_to(xn_ref, h_ref, g1_ref[...], rows, zero_ref=o_ref)

    xn = xn_ref[...]
    gate = jnp.dot(xn, wg_ref[...].astype(BF16), preferred_element_type=F32)
    up = jnp.dot(xn, wu_ref[...].astype(BF16), preferred_element_type=F32)
    repeated = f * FF_TILE - _ff_offset(f)
    fresh = lax.broadcasted_iota(jnp.int32, (1, FF_TILE), 1) >= repeated
    hid = jnp.where(fresh, jax.nn.silu(gate) * up, 0.0).astype(BF16)
    o_ref[...] += jnp.dot(hid, wo_ref[...].astype(BF16), preferred_element_type=F32)

    @pl.when(f == pl.num_programs(1) - 1)
    def _():
        _residual_norm_rows(o_ref, h_ref, 0.5 * g2_ref[...], rows)


def _ffn(h, ln, w_in_all, w_out_all, layer, half):
    m = h.shape[0]
    elem = pl.Element
    return pl.pallas_call(
        _ffn_kernel,
        out_shape=jax.ShapeDtypeStruct((m, D_MODEL), F32),
        grid=(m // ROW_TILE, pl.cdiv(D_FF, FF_TILE)),
        in_specs=[
            pl.BlockSpec((ROW_TILE, D_MODEL), lambda i, f: (i, 0)),
            pl.BlockSpec((1, D_MODEL), lambda i, f: (0, 0)),
            pl.BlockSpec((1, D_MODEL), lambda i, f: (0, 0)),
            pl.BlockSpec((None, None, elem(D_MODEL), elem(FF_TILE)),
                         lambda i, f: (layer, half, 0, _ff_offset(f))),
            pl.BlockSpec((None, None, elem(D_MODEL), elem(FF_TILE)),
                         lambda i, f: (layer, half, 0, _ff_offset(f, D_FF))),
            pl.BlockSpec((None, None, elem(FF_TILE), elem(D_MODEL)),
                         lambda i, f: (layer, half, _ff_offset(f), 0)),
        ],
        out_specs=pl.BlockSpec((ROW_TILE, D_MODEL), lambda i, f: (i, 0)),
        scratch_shapes=[pltpu.VMEM((ROW_TILE, D_MODEL), BF16)],
        compiler_params=_params("parallel", "arbitrary"),
        name="ffn",
    )(h, ln[0:1], ln[1:2], w_in_all, w_in_all, w_out_all)


def _pool_kernel(h_ref, halo_ref, gin_ref, gout_ref, w_ref, scale_ref, o_ref):
    t = pl.program_id(1)
    rows = h_ref.shape[1]
    h = h_ref[0]
    gin = gin_ref[...]
    u = _rms(h, gin)
    has_prev = (t > 0).astype(F32)
    halo = _rms(halo_ref[0], gin) * has_prev
    hist = jnp.concatenate([halo, u], axis=0)
    hist_hi = hist.astype(BF16)
    hist_lo = (hist - hist_hi.astype(F32)).astype(BF16)
    qi = lax.broadcasted_iota(jnp.int32, (rows, rows + POOL_HALO), 0) + POOL_HALO
    kj = lax.broadcasted_iota(jnp.int32, (rows, rows + POOL_HALO), 1)
    back = qi - kj
    seq_pos = t * rows + lax.broadcasted_iota(jnp.int32, (rows, 1), 0)
    outs = []
    for g, win in enumerate(POOL_WINDOWS):
        cols = slice(g * POOL_GROUP, (g + 1) * POOL_GROUP)
        band = jnp.where((back >= 1) & (back < win), 1.0, 0.0).astype(BF16)
        others = (jnp.dot(band, hist_hi[:, cols], preferred_element_type=F32)
                  + jnp.dot(band, hist_lo[:, cols], preferred_element_type=F32))
        count = jnp.minimum(seq_pos + 1, win).astype(F32)
        own = u[:, cols]
        mixed = ((others + own) / count - own).astype(BF16)
        y = jnp.dot(mixed, w_ref[g], preferred_element_type=F32)
        outs.append(y * scale_ref[:, cols])
    y = jnp.concatenate(outs, axis=1)
    o_ref[0] = h + _rms(y, gout_ref[...])


def _pool_mixer(h, ln, w_group, scale, batch, seq):
    h3 = h.reshape(batch, seq, D_MODEL)
    halo_blocks = POOL_ROWS // POOL_HALO
    out = pl.pallas_call(
        _pool_kernel,
        out_shape=jax.ShapeDtypeStruct(h3.shape, F32),
        grid=(batch, seq // POOL_ROWS),
        in_specs=[
            pl.BlockSpec((1, POOL_ROWS, D_MODEL), lambda b, t: (b, t, 0)),
            pl.BlockSpec((1, POOL_HALO, D_MODEL),
                         lambda b, t: (b, jnp.maximum(t * halo_blocks - 1, 0), 0)),
            pl.BlockSpec((1, D_MODEL), lambda b, t: (0, 0)),
            pl.BlockSpec((1, D_MODEL), lambda b, t: (0, 0)),
            pl.BlockSpec((len(POOL_WINDOWS), POOL_GROUP, POOL_GROUP), lambda b, t: (0, 0, 0)),
            pl.BlockSpec((1, D_MODEL), lambda b, t: (0, 0)),
        ],
        out_specs=pl.BlockSpec((1, POOL_ROWS, D_MODEL), lambda b, t: (b, t, 0)),
        compiler_params=_params("parallel", "parallel"),
        name="pool_mixer",
    )(h3, h3, ln[0:1], ln[1:2], w_group.astype(BF16), scale.reshape(1, D_MODEL))
    return out.reshape(batch * seq, D_MODEL)


def _proj_in_kernel(h_ref, g_ref, pos_ref, freq_ref, w_ref, b_ref, o_ref, xn_ref, rot_a_ref, rot_b_ref,
                    *, mode, q_cols, k_cols, q_scale, k_scale):
    j = pl.program_id(1)
    rows, col_tile = o_ref.shape
    lane = lax.broadcasted_iota(jnp.int32, (1, LANES), 1)
    first_half = (lane % SWA_HD) < (SWA_HD // 2)

    @pl.when(j == 0)
    def _():
        _norm_rows_to(xn_ref, h_ref, g_ref[...], rows)
        ang = pos_ref[...].astype(F32) * freq_ref[...]
        cos = jnp.cos(ang)
        sin = jnp.sin(ang)
        if mode == "lane":
            sin = jnp.where(first_half, -sin, sin)
        rot_a_ref[0] = cos * q_scale
        rot_b_ref[0] = sin * q_scale
        rot_a_ref[1] = cos * k_scale
        rot_b_ref[1] = sin * k_scale
        rot_a_ref[2] = jnp.ones_like(cos)
        rot_b_ref[2] = jnp.zeros_like(sin)

    y = jnp.dot(xn_ref[...], w_ref[...].astype(BF16), preferred_element_type=F32) + b_ref[...]
    unit = 2 * LANES if mode == "pairs" else LANES
    outs = []
    for c in range(col_tile // unit):
        col = j * col_tile + c * unit
        kind = jnp.where(col < q_cols, 0, jnp.where(col < q_cols + k_cols, 1, 2))
        a = rot_a_ref[kind]
        b = rot_b_ref[kind]
        if mode == "pairs":
            x1 = y[:, c * unit:c * unit + LANES]
            x2 = y[:, c * unit + LANES:(c + 1) * unit]
            outs += [x1 * a - x2 * b, x2 * a + x1 * b]
        else:
            half = SWA_HD // 2
            xb = y[:, c * unit:(c + 1) * unit]
            partner = jnp.where(first_half, pltpu.roll(xb, LANES - half, axis=1), pltpu.roll(xb, half, axis=1))
            outs.append(xb * a + partner * b)
    o_ref[...] = jnp.concatenate(outs, axis=1).astype(o_ref.dtype)


def _proj_in(h, g, pos, freq_row, w, bias, *, col_tile, mode, q_cols, k_cols, q_scale, k_scale):
    m = h.shape[0]
    n = w.shape[1]
    kern = functools.partial(_proj_in_kernel, mode=mode, q_cols=q_cols, k_cols=k_cols,
                             q_scale=q_scale, k_scale=k_scale)
    return pl.pallas_call(
        kern,
        out_shape=jax.ShapeDtypeStruct((m, n), BF16),
        grid=(m // ROW_TILE, n // col_tile),
        in_specs=[
            pl.BlockSpec((ROW_TILE, D_MODEL), lambda i, j: (i, 0)),
            pl.BlockSpec((1, D_MODEL), lambda i, j: (0, 0)),
            pl.BlockSpec((ROW_TILE, 1), lambda i, j: (i, 0)),
            pl.BlockSpec((1, LANES), lambda i, j: (0, 0)),
            pl.BlockSpec((D_MODEL, col_tile), lambda i, j: (0, j)),
            pl.BlockSpec((1, col_tile), lambda i, j: (0, j)),
        ],
        out_specs=pl.BlockSpec((ROW_TILE, col_tile), lambda i, j: (i, j)),
        scratch_shapes=[pltpu.VMEM((ROW_TILE, D_MODEL), BF16),
                        pltpu.VMEM((3, ROW_TILE, LANES), F32),
                        pltpu.VMEM((3, ROW_TILE, LANES), F32)],
        compiler_params=_params("parallel", "arbitrary"),
        name="proj_in_" + mode,
    )(h, g, pos, freq_row, w, bias.reshape(1, n))


def _proj_out_kernel(x_ref, w_ref, b_ref, g_ref, h_ref, o_ref):
    k = pl.program_id(1)
    rows = h_ref.shape[0]

    def partial_product():
        return jnp.dot(x_ref[...], w_ref[...].astype(BF16), preferred_element_type=F32)

    @pl.when(k == 0)
    def _():
        o_ref[...] = partial_product() + b_ref[...]

    @pl.when(k > 0)
    def _():
        o_ref[...] += partial_product()

    @pl.when(k == pl.num_programs(1) - 1)
    def _():
        _residual_norm_rows(o_ref, h_ref, g_ref[...], rows)


def _proj_out(x, w, bias, g, h):
    m, kdim = x.shape
    return pl.pallas_call(
        _proj_out_kernel,
        out_shape=jax.ShapeDtypeStruct((m, D_MODEL), F32),
        grid=(m // ROW_TILE, kdim // PROJ_OUT_K),
        in_specs=[
            pl.BlockSpec((ROW_TILE, PROJ_OUT_K), lambda i, k: (i, k)),
            pl.BlockSpec((PROJ_OUT_K, D_MODEL), lambda i, k: (k, 0)),
            pl.BlockSpec((1, D_MODEL), lambda i, k: (0, 0)),
            pl.BlockSpec((1, D_MODEL), lambda i, k: (0, 0)),
            pl.BlockSpec((ROW_TILE, D_MODEL), lambda i, k: (i, 0)),
        ],
        out_specs=pl.BlockSpec((ROW_TILE, D_MODEL), lambda i, k: (i, 0)),
        compiler_params=_params("parallel", "arbitrary"),
        name="proj_out",
    )(x, w, bias.reshape(1, D_MODEL), g, h)


def _retention_kernel(lg_ref, q_ref, k_ref, v_ref, gate_ref, gng_ref, gnb_ref, o_ref, state_ref):
    group = pl.program_id(1)
    chunk = q_ref.shape[1]

    @pl.when(pl.program_id(2) == 0)
    def _():
        state_ref[...] = jnp.zeros_like(state_ref)

    qi = lax.broadcasted_iota(jnp.int32, (chunk, chunk), 0)
    kj = lax.broadcasted_iota(jnp.int32, (chunk, chunk), 1)
    rel = (qi - kj).astype(F32)
    idx = lax.broadcasted_iota(jnp.int32, (chunk, 1), 0).astype(F32)

    for t in range(RET_HEADS_PER_STEP):
        log_gamma = jnp.full((1, 1), lg_ref[group * RET_HEADS_PER_STEP + t], F32)
        qk_cols = slice(t * RET_DK, (t + 1) * RET_DK)
        v_cols = slice(t * RET_DV, (t + 1) * RET_DV)
        q = q_ref[0, :, qk_cols]
        k = k_ref[0, :, qk_cols]
        v = v_ref[0, :, v_cols]
        inner_decay = jnp.where(rel >= 0, jnp.exp(jnp.maximum(rel, 0.0) * log_gamma), 0.0)
        cross_decay = jnp.exp((idx + 1.0) * log_gamma)
        state_decay = jnp.exp((chunk - 1.0 - idx) * log_gamma)
        chunk_decay = jnp.exp(float(chunk) * log_gamma)

        scores = lax.dot_general(q, k, (((1,), (1,)), ((), ())), preferred_element_type=F32) * inner_decay
        inner = jnp.dot(scores.astype(BF16), v, preferred_element_type=F32)
        state = state_ref[t]
        cross = jnp.dot(q, state.astype(BF16), preferred_element_type=F32) * cross_decay
        k_decayed = (k.astype(F32) * state_decay).astype(BF16)
        state_ref[t] = state * chunk_decay + lax.dot_general(
            k_decayed, v, (((0,), (0,)), ((), ())), preferred_element_type=F32)

        o = inner + cross
        mu = jnp.mean(o, axis=-1, keepdims=True)
        var = jnp.mean(jnp.square(o - mu), axis=-1, keepdims=True)
        o = (o - mu) * lax.rsqrt(var + EPS)
        o = o * gng_ref[:, v_cols] + gnb_ref[:, v_cols]
        o_ref[0, :, v_cols] = (jax.nn.silu(gate_ref[0, :, v_cols].astype(F32)) * o).astype(o_ref.dtype)


def _retention_core(hproj, gn_g, gn_b, batch, seq):
    hp = hproj.reshape(batch, seq, 6 * D_MODEL)
    log_gamma = jnp.log(1.0 - 2.0 ** (-5.0 - jnp.arange(RET_HEADS, dtype=F32)))
    qk_w = RET_HEADS_PER_STEP * RET_DK
    v_w = RET_HEADS_PER_STEP * RET_DV
    k_blk0 = D_MODEL // qk_w
    v_blk0 = 2 * D_MODEL // v_w
    g_blk0 = 4 * D_MODEL // v_w
    out = pl.pallas_call(
        _retention_kernel,
        out_shape=jax.ShapeDtypeStruct((batch, seq, 2 * D_MODEL), BF16),
        grid=(batch, RET_HEADS // RET_HEADS_PER_STEP, seq // RET_CHUNK),
        in_specs=[
            pl.BlockSpec(memory_space=pltpu.SMEM),
            pl.BlockSpec((1, RET_CHUNK, qk_w), lambda b, hg, c: (b, c, hg)),
            pl.BlockSpec((1, RET_CHUNK, qk_w), lambda b, hg, c: (b, c, k_blk0 + hg)),
            pl.BlockSpec((1, RET_CHUNK, v_w), lambda b, hg, c: (b, c, v_blk0 + hg)),
            pl.BlockSpec((1, RET_CHUNK, v_w), lambda b, hg, c: (b, c, g_blk0 + hg)),
            pl.BlockSpec((1, v_w), lambda b, hg, c: (0, hg)),
            pl.BlockSpec((1, v_w), lambda b, hg, c: (0, hg)),
        ],
        out_specs=pl.BlockSpec((1, RET_CHUNK, v_w), lambda b, hg, c: (b, c, hg)),
        scratch_shapes=[pltpu.VMEM((RET_HEADS_PER_STEP, RET_DK, RET_DV), F32)],
        compiler_params=_params("parallel", "parallel", "arbitrary"),
        name="retention_core",
    )(log_gamma, hp, hp, hp, hp, gn_g.reshape(1, 2 * D_MODEL), gn_b.reshape(1, 2 * D_MODEL))
    return out.reshape(batch * seq, 2 * D_MODEL)


def _swa_kernel(sink_ref, q_ref, kc_ref, vc_ref, kp_ref, vp_ref, o_ref):
    n = pl.program_id(1)
    blk = SWA_BLOCK
    lane = lax.broadcasted_iota(jnp.int32, (1, LANES), 1)
    low = lane < SWA_HD

    def both_halves(t, odd):
        t = t.astype(F32)
        keep = jnp.logical_not(low) if odd else low
        return jnp.where(keep, t, pltpu.roll(t, SWA_HD, axis=1)).astype(BF16)

    k_all = jnp.concatenate([kp_ref[0], kc_ref[0]], axis=0)
    v_all = jnp.concatenate([vp_ref[0], vc_ref[0]], axis=0)
    qi = lax.broadcasted_iota(jnp.int32, (blk, 2 * blk), 0)
    kj = lax.broadcasted_iota(jnp.int32, (blk, 2 * blk), 1)
    diff = qi + blk - kj
    first_key = jnp.where(n > 0, 0, blk)
    allowed = (diff >= 0) & (diff < SWA_WINDOW) & (kj >= first_key)

    for kv in range(SWA_KV_HEADS):
        kv_cols = slice((kv // 2) * LANES, (kv // 2 + 1) * LANES)
        k2 = both_halves(k_all[:, kv_cols], kv % 2)
        v2 = both_halves(v_all[:, kv_cols], kv % 2)
        outs = []
        for jh in range(SWA_GROUP):
            head = kv * SWA_GROUP + jh
            qp = q_ref[0, :, (head // 2) * LANES:(head // 2 + 1) * LANES].astype(F32)
            keep = jnp.logical_not(low) if head % 2 else low
            qh = jnp.where(keep, qp, 0.0).astype(BF16)
            sink = jnp.full((blk, 1), sink_ref[head], F32)
            s = lax.dot_general(qh, k2, (((1,), (1,)), ((), ())), preferred_element_type=F32)
            s = jnp.where(allowed, s, -jnp.inf)
            m = jnp.maximum(jnp.max(s, axis=-1, keepdims=True), sink)
            e = jnp.exp(s - m)
            denom = jnp.sum(e, axis=-1, keepdims=True) + jnp.exp(sink - m)
            ov = jnp.dot(e.astype(BF16), v2, preferred_element_type=F32)
            outs.append(ov / denom)
        for jp in range(SWA_GROUP // 2):
            c = kv * (SWA_GROUP // 2) + jp
            o_ref[0, :, c * LANES:(c + 1) * LANES] = jnp.where(low, outs[2 * jp], outs[2 * jp + 1]).astype(
                o_ref.dtype)


def _swa_core(hproj, sinks, batch, seq):
    width = hproj.shape[1]
    hp = hproj.reshape(batch, seq, width)
    q_width = SWA_HEADS * SWA_HD
    kv_width = SWA_KV_HEADS * SWA_HD
    k_blk = q_width // kv_width
    v_blk = k_blk + 1
    prev = lambda n: jnp.maximum(n - 1, 0)
    out = pl.pallas_call(
        _swa_kernel,
        out_shape=jax.ShapeDtypeStruct((batch, seq, q_width), BF16),
        grid=(batch, seq // SWA_BLOCK),
        in_specs=[
            pl.BlockSpec(memory_space=pltpu.SMEM),
            pl.BlockSpec((1, SWA_BLOCK, q_width), lambda b, n: (b, n, 0)),
            pl.BlockSpec((1, SWA_BLOCK, kv_width), lambda b, n: (b, n, k_blk)),
            pl.BlockSpec((1, SWA_BLOCK, kv_width), lambda b, n: (b, n, v_blk)),
            pl.BlockSpec((1, SWA_BLOCK, kv_width), lambda b, n: (b, prev(n), k_blk)),
            pl.BlockSpec((1, SWA_BLOCK, kv_width), lambda b, n: (b, prev(n), v_blk)),
        ],
        out_specs=pl.BlockSpec((1, SWA_BLOCK, q_width), lambda b, n: (b, n, 0)),
        compiler_params=_params("parallel", "parallel"),
        name="swa_core",
    )(sinks, hp, hp, hp, hp, hp)
    return out.reshape(batch * seq, q_width)


def _retention_mixer(h, ln, pos, w_in, gn_g, gn_b, w_out, batch, seq):
    inv_freq = ROPE_THETA ** (-jnp.linspace(0.0, 1.0, RET_DK // 2, dtype=F32))
    hproj = _proj_in(
        h, ln[0:1], pos, inv_freq.reshape(1, LANES), w_in, jnp.zeros((w_in.shape[1],), F32),
        col_tile=RET_COL_TILE, mode="pairs", q_cols=D_MODEL, k_cols=D_MODEL, q_scale=1.0,
        k_scale=RET_DK ** -0.5)
    gated = _retention_core(hproj, gn_g, gn_b, batch, seq)
    return _proj_out(gated, w_out, jnp.zeros((D_MODEL,), F32), ln[1:2], h)


def _swa_mixer(h, ln, pos, w_in, b_in, sinks, w_out, b_out, batch, seq):
    inv_freq = ROPE_THETA ** (-jnp.arange(0, SWA_HD, 2, dtype=F32) / SWA_HD)
    freq_row = jnp.tile(inv_freq, LANES // (SWA_HD // 2)).reshape(1, LANES)
    hproj = _proj_in(
        h, ln[0:1], pos, freq_row, w_in, b_in,
        col_tile=SWA_COL_TILE, mode="lane", q_cols=SWA_HEADS * SWA_HD, k_cols=SWA_KV_HEADS * SWA_HD,
        q_scale=SWA_HD ** -0.5, k_scale=1.0)
    attn = _swa_core(hproj, sinks, batch, seq)
    return _proj_out(attn, w_out, b_out, ln[1:2], h)


def kernel(x, positions, ln_ffn1, ln_mix, ln_ffn2, ffn_w_in, ffn_w_out, pool_w, pool_scale, ret_w_in,
           ret_gn_g, ret_gn_b, ret_w_out, swa_w_in, swa_b_in, swa_sinks, swa_w_out, swa_b_out):
    batch, seq, _ = x.shape
    h = x.reshape(batch * seq, D_MODEL)
    pos = positions.reshape(batch * seq, 1)
    for i in range(DEPTH):
        h = _ffn(h, ln_ffn1[i], ffn_w_in, ffn_w_out, i, 0)
        kind, j = i % N_MIXERS, i // N_MIXERS
        if kind == 0:
            h = _pool_mixer(h, ln_mix[i], pool_w[j], pool_scale[j], batch, seq)
        elif kind == 1:
            h = _retention_mixer(h, ln_mix[i], pos, ret_w_in[j], ret_gn_g[j], ret_gn_b[j], ret_w_out[j],
                                 batch, seq)
        else:
            h = _swa_mixer(h, ln_mix[i], pos, swa_w_in[j], swa_b_in[j], swa_sinks[j], swa_w_out[j],
                           swa_b_out[j], batch, seq)
        h = _ffn(h, ln_ffn2[i], ffn_w_in, ffn_w_out, i, 1)
    return h.reshape(batch, seq, D_MODEL)
```

```python
import functools

import jax
import jax.numpy as jnp
from jax import lax
from jax.experimental import pallas as pl
from jax.experimental.pallas import tpu as pltpu

F32 = jnp.float32
BF16 = jnp.bfloat16

D_MODEL = 2048
DEPTH = 4
N_MIXERS = 3
EPS = 1e-6
D_FF = 5504
POOL_WINDOWS = (2, 4, 8, 16)
POOL_GROUP = D_MODEL // len(POOL_WINDOWS)
POOL_HALO = 16
RET_HEADS = 8
RET_DK = D_MODEL // RET_HEADS
RET_DV = 2 * D_MODEL // RET_HEADS
SWA_HEADS = 32
SWA_KV_HEADS = 4
SWA_HD = 64
SWA_GROUP = SWA_HEADS // SWA_KV_HEADS
SWA_WINDOW = 128
SWA_BLOCK = 128
ROPE_THETA = 10000.0
LOG2_E = 1.4426950408889634

LANES = 128
VMEM_LIMIT_BYTES = 58 * 1024 * 1024

ROW_TILE = 1024
NORM_ROWS = 128
FF_TILE = 256
FF_TAIL = D_FF % FF_TILE or FF_TILE
PROJ_OUT_K = 1024
POOL_ROWS = 256
POOL_SUB = 128
RET_COL_TILE = 1024
SWA_COL_TILE = 512
RET_HEADS_PER_STEP = 8
RET_CHUNK = 256


def _rms(x, g):
    return x * lax.rsqrt(jnp.mean(x * x, axis=-1, keepdims=True) + EPS) * g


def _params(*sem):
    return pltpu.CompilerParams(dimension_semantics=sem, vmem_limit_bytes=VMEM_LIMIT_BYTES)


def _norm_rows_to(dst_ref, src_ref, g, rows, zero_ref=None):
    def body(i, carry):
        r = pl.ds(pl.multiple_of(i * NORM_ROWS, NORM_ROWS), NORM_ROWS)
        dst_ref[r, :] = _rms(src_ref[r, :], g).astype(dst_ref.dtype)
        if zero_ref is not None:
            zero_ref[r, :] = jnp.zeros((NORM_ROWS, zero_ref.shape[1]), zero_ref.dtype)
        return carry
    lax.fori_loop(0, rows // NORM_ROWS, body, 0)


def _residual_norm_rows(o_ref, h_ref, g, rows):
    def body(i, carry):
        r = pl.ds(pl.multiple_of(i * NORM_ROWS, NORM_ROWS), NORM_ROWS)
        o_ref[r, :] = h_ref[r, :] + _rms(o_ref[r, :], g)
        return carry
    lax.fori_loop(0, rows // NORM_ROWS, body, 0)


def _ff_offset(f, base=0):
    return LANES * (base // LANES + jnp.minimum(f * (FF_TILE // LANES), (D_FF - FF_TILE) // LANES))


def _ffn_kernel(h_ref, g1_ref, g2_ref, wg_ref, wu_ref, wo_ref, o_ref, xn_ref):
    f = pl.program_id(1)
    last = pl.num_programs(1) - 1
    rows = h_ref.shape[0]

    @pl.when(f == 0)
    def _():
        _norm_rows_to(xn_ref, h_ref, g1_ref[...], rows, zero_ref=o_ref)

    def step(width):
        xn = xn_ref[...]
        wg = wg_ref[:, FF_TILE - width:].astype(BF16)
        wu = wu_ref[:, FF_TILE - width:].astype(BF16)
        if width == FF_TILE:
            gate = jnp.dot(xn, wg, preferred_element_type=F32)
            up = jnp.dot(xn, wu, preferred_element_type=F32)
        else:
            both = jnp.dot(xn, jnp.concatenate([wg, wu], axis=1), preferred_element_type=F32)
            gate, up = both[:, :width], both[:, width:]
        hid = (jax.nn.silu(gate) * up).astype(BF16)
        o_ref[...] += jnp.dot(hid, wo_ref[FF_TILE - width:, :].astype(BF16), preferred_element_type=F32)

    @pl.when(f < last)
    def _():
        step(FF_TILE)

    @pl.when(f == last)
    def _():
        step(FF_TAIL)
        _residual_norm_rows(o_ref, h_ref, 0.5 * g2_ref[...], rows)


def _ffn(h, ln, w_in_all, w_out_all, layer, half):
    m = h.shape[0]
    elem = pl.Element
    return pl.pallas_call(
        _ffn_kernel,
        out_shape=jax.ShapeDtypeStruct((m, D_MODEL), F32),
        grid=(m // ROW_TILE, pl.cdiv(D_FF, FF_TILE)),
        in_specs=[
            pl.BlockSpec((ROW_TILE, D_MODEL), lambda i, f: (i, 0)),
            pl.BlockSpec((1, D_MODEL), lambda i, f: (0, 0)),
            pl.BlockSpec((1, D_MODEL), lambda i, f: (0, 0)),
            pl.BlockSpec((None, None, elem(D_MODEL), elem(FF_TILE)),
                         lambda i, f: (layer, half, 0, _ff_offset(f))),
            pl.BlockSpec((None, None, elem(D_MODEL), elem(FF_TILE)),
                         lambda i, f: (layer, half, 0, _ff_offset(f, D_FF))),
            pl.BlockSpec((None, None, elem(FF_TILE), elem(D_MODEL)),
                         lambda i, f: (layer, half, _ff_offset(f), 0)),
        ],
        out_specs=pl.BlockSpec((ROW_TILE, D_MODEL), lambda i, f: (i, 0)),
        scratch_shapes=[pltpu.VMEM((ROW_TILE, D_MODEL), BF16)],
        compiler_params=_params("parallel", "arbitrary"),
        name="ffn",
    )(h, ln[0:1], ln[1:2], w_in_all, w_in_all, w_out_all)


def _pool_kernel(h_ref, halo_ref, gin_ref, gout_ref, band_ref, w_ref, scale_ref, o_ref):
    t = pl.program_id(1)
    rows = h_ref.shape[1]
    h = h_ref[0]
    gin = gin_ref[...]
    u = _rms(h, gin)
    has_prev = (t > 0).astype(F32)
    halo = _rms(halo_ref[0], gin) * has_prev
    hist = jnp.concatenate([halo, u], axis=0)
    hist_hi = hist.astype(BF16)
    hist_lo = (hist - hist_hi.astype(F32)).astype(BF16)
    seq_pos = t * rows + lax.broadcasted_iota(jnp.int32, (rows, 1), 0)
    outs = []
    for g, win in enumerate(POOL_WINDOWS):
        cols = slice(g * POOL_GROUP, (g + 1) * POOL_GROUP)
        band = band_ref[g]
        parts = []
        for sb in range(rows // POOL_SUB):
            hrows = slice(sb * POOL_SUB, (sb + 1) * POOL_SUB + POOL_HALO)
            parts.append(jnp.dot(band, hist_hi[hrows, cols], preferred_element_type=F32)
                         + jnp.dot(band, hist_lo[hrows, cols], preferred_element_type=F32))
        others = jnp.concatenate(parts, axis=0)
        count = jnp.minimum(seq_pos + 1, win).astype(F32)
        own = u[:, cols]
        mixed = ((others + own) / count - own).astype(BF16)
        y = jnp.dot(mixed, w_ref[g], preferred_element_type=F32)
        outs.append(y * scale_ref[:, cols])
    y = jnp.concatenate(outs, axis=1)
    o_ref[0] = h + _rms(y, gout_ref[...])


def _pool_mixer(h, ln, w_group, scale, batch, seq):
    h3 = h.reshape(batch, seq, D_MODEL)
    halo_blocks = POOL_ROWS // POOL_HALO
    back = (jnp.arange(POOL_SUB)[:, None] + POOL_HALO) - jnp.arange(POOL_SUB + POOL_HALO)[None, :]
    band = jnp.stack([(back >= 1) & (back < win) for win in POOL_WINDOWS]).astype(BF16)
    out = pl.pallas_call(
        _pool_kernel,
        out_shape=jax.ShapeDtypeStruct(h3.shape, F32),
        grid=(batch, seq // POOL_ROWS),
        in_specs=[
            pl.BlockSpec((1, POOL_ROWS, D_MODEL), lambda b, t: (b, t, 0)),
            pl.BlockSpec((1, POOL_HALO, D_MODEL),
                         lambda b, t: (b, jnp.maximum(t * halo_blocks - 1, 0), 0)),
            pl.BlockSpec((1, D_MODEL), lambda b, t: (0, 0)),
            pl.BlockSpec((1, D_MODEL), lambda b, t: (0, 0)),
            pl.BlockSpec(band.shape, lambda b, t: (0, 0, 0)),
            pl.BlockSpec((len(POOL_WINDOWS), POOL_GROUP, POOL_GROUP), lambda b, t: (0, 0, 0)),
            pl.BlockSpec((1, D_MODEL), lambda b, t: (0, 0)),
        ],
        out_specs=pl.BlockSpec((1, POOL_ROWS, D_MODEL), lambda b, t: (b, t, 0)),
        compiler_params=_params("parallel", "parallel"),
        name="pool_mixer",
    )(h3, h3, ln[0:1], ln[1:2], band, w_group.astype(BF16), scale.reshape(1, D_MODEL))
    return out.reshape(batch * seq, D_MODEL)


def _proj_in_kernel(h_ref, g_ref, pos_ref, freq_ref, w_ref, b_ref, o_ref, xn_ref, rot_a_ref, rot_b_ref,
                    *, mode, q_cols, k_cols, q_scale, k_scale):
    j = pl.program_id(1)
    rows, col_tile = o_ref.shape
    lane = lax.broadcasted_iota(jnp.int32, (1, LANES), 1)
    first_half = (lane % SWA_HD) < (SWA_HD // 2)

    @pl.when(j == 0)
    def _():
        _norm_rows_to(xn_ref, h_ref, g_ref[...], rows)
        ang = pos_ref[...].astype(F32) * freq_ref[...]
        cos = jnp.cos(ang)
        sin = jnp.sin(ang)
        if mode == "lane":
            sin = jnp.where(first_half, -sin, sin)
        rot_a_ref[0] = cos * q_scale
        rot_b_ref[0] = sin * q_scale
        rot_a_ref[1] = cos * k_scale
        rot_b_ref[1] = sin * k_scale
        rot_a_ref[2] = jnp.ones_like(cos)
        rot_b_ref[2] = jnp.zeros_like(sin)

    y = jnp.dot(xn_ref[...], w_ref[...].astype(BF16), preferred_element_type=F32) + b_ref[...]
    unit = 2 * LANES if mode == "pairs" else LANES
    outs = []
    for c in range(col_tile // unit):
        col = j * col_tile + c * unit
        kind = jnp.where(col < q_cols, 0, jnp.where(col < q_cols + k_cols, 1, 2))
        a = rot_a_ref[kind]
        b = rot_b_ref[kind]
        if mode == "pairs":
            x1 = y[:, c * unit:c * unit + LANES]
            x2 = y[:, c * unit + LANES:(c + 1) * unit]
            outs += [x1 * a - x2 * b, x2 * a + x1 * b]
        else:
            half = SWA_HD // 2
            xb = y[:, c * unit:(c + 1) * unit]
            partner = jnp.where(first_half, pltpu.roll(xb, LANES - half, axis=1), pltpu.roll(xb, half, axis=1))
            outs.append(xb * a + partner * b)
    o_ref[...] = jnp.concatenate(outs, axis=1).astype(o_ref.dtype)


def _proj_in(h, g, pos, freq_row, w, bias, *, col_tile, mode, q_cols, k_cols, q_scale, k_scale):
    m = h.shape[0]
    n = w.shape[1]
    kern = functools.partial(_proj_in_kernel, mode=mode, q_cols=q_cols, k_cols=k_cols,
                             q_scale=q_scale, k_scale=k_scale)
    return pl.pallas_call(
        kern,
        out_shape=jax.ShapeDtypeStruct((m, n), BF16),
        grid=(m // ROW_TILE, n // col_tile),
        in_specs=[
            pl.BlockSpec((ROW_TILE, D_MODEL), lambda i, j: (i, 0)),
            pl.BlockSpec((1, D_MODEL), lambda i, j: (0, 0)),
            pl.BlockSpec((ROW_TILE, 1), lambda i, j: (i, 0)),
            pl.BlockSpec((1, LANES), lambda i, j: (0, 0)),
            pl.BlockSpec((D_MODEL, col_tile), lambda i, j: (0, j)),
            pl.BlockSpec((1, col_tile), lambda i, j: (0, j)),
        ],
        out_specs=pl.BlockSpec((ROW_TILE, col_tile), lambda i, j: (i, j)),
        scratch_shapes=[pltpu.VMEM((ROW_TILE, D_MODEL), BF16),
                        pltpu.VMEM((3, ROW_TILE, LANES), F32),
                        pltpu.VMEM((3, ROW_TILE, LANES), F32)],
        compiler_params=_params("parallel", "arbitrary"),
        name="proj_in_" + mode,
    )(h, g, pos, freq_row, w, bias.reshape(1, n))


def _proj_out_kernel(x_ref, w_ref, b_ref, g_ref, h_ref, o_ref):
    k = pl.program_id(1)
    rows = h_ref.shape[0]

    def partial_product():
        return jnp.dot(x_ref[...], w_ref[...].astype(BF16), preferred_element_type=F32)

    @pl.when(k == 0)
    def _():
        o_ref[...] = partial_product() + b_ref[...]

    @pl.when(k > 0)
    def _():
        o_ref[...] += partial_product()

    @pl.when(k == pl.num_programs(1) - 1)
    def _():
        _residual_norm_rows(o_ref, h_ref, g_ref[...], rows)


def _proj_out(x, w, bias, g, h):
    m, kdim = x.shape
    return pl.pallas_call(
        _proj_out_kernel,
        out_shape=jax.ShapeDtypeStruct((m, D_MODEL), F32),
        grid=(m // ROW_TILE, kdim // PROJ_OUT_K),
        in_specs=[
            pl.BlockSpec((ROW_TILE, PROJ_OUT_K), lambda i, k: (i, k)),
            pl.BlockSpec((PROJ_OUT_K, D_MODEL), lambda i, k: (k, 0)),
            pl.BlockSpec((1, D_MODEL), lambda i, k: (0, 0)),
            pl.BlockSpec((1, D_MODEL), lambda i, k: (0, 0)),
            pl.BlockSpec((ROW_TILE, D_MODEL), lambda i, k: (i, 0)),
        ],
        out_specs=pl.BlockSpec((ROW_TILE, D_MODEL), lambda i, k: (i, 0)),
        compiler_params=_params("parallel", "arbitrary"),
        name="proj_out",
    )(x, w, bias.reshape(1, D_MODEL), g, h)


def _retention_kernel(lg_ref, q_ref, k_ref, v_ref, gate_ref, gng_ref, gnb_ref, o_ref, state_ref):
    group = pl.program_id(1)
    chunk = q_ref.shape[1]

    @pl.when(pl.program_id(2) == 0)
    def _():
        state_ref[...] = jnp.zeros_like(state_ref)

    qi = lax.broadcasted_iota(jnp.int32, (chunk, chunk), 0)
    kj = lax.broadcasted_iota(jnp.int32, (chunk, chunk), 1)
    rel = (qi - kj).astype(F32)
    idx = lax.broadcasted_iota(jnp.int32, (chunk, 1), 0).astype(F32)

    for t in range(RET_HEADS_PER_STEP):
        log_gamma = jnp.full((1, 1), lg_ref[group * RET_HEADS_PER_STEP + t], F32)
        qk_cols = slice(t * RET_DK, (t + 1) * RET_DK)
        v_cols = slice(t * RET_DV, (t + 1) * RET_DV)
        q = q_ref[0, :, qk_cols]
        k = k_ref[0, :, qk_cols]
        v = v_ref[0, :, v_cols]
        inner_decay = jnp.where(rel >= 0, jnp.exp(jnp.maximum(rel, 0.0) * log_gamma), 0.0)
        cross_decay = jnp.exp((idx + 1.0) * log_gamma)
        state_decay = jnp.exp((chunk - 1.0 - idx) * log_gamma)
        chunk_decay = jnp.exp(float(chunk) * log_gamma)

        scores = lax.dot_general(q, k, (((1,), (1,)), ((), ())), preferred_element_type=F32) * inner_decay
        inner = jnp.dot(scores.astype(BF16), v, preferred_element_type=F32)
        state = state_ref[t]
        cross = jnp.dot(q, state.astype(BF16), preferred_element_type=F32) * cross_decay
        k_decayed = (k.astype(F32) * state_decay).astype(BF16)
        state_ref[t] = state * chunk_decay + lax.dot_general(
            k_decayed, v, (((0,), (0,)), ((), ())), preferred_element_type=F32)

        o = inner + cross
        mu = jnp.mean(o, axis=-1, keepdims=True)
        var = jnp.mean(jnp.square(o - mu), axis=-1, keepdims=True)
        o = (o - mu) * lax.rsqrt(var + EPS)
        o = o * gng_ref[:, v_cols] + gnb_ref[:, v_cols]
        o_ref[0, :, v_cols] = (jax.nn.silu(gate_ref[0, :, v_cols].astype(F32)) * o).astype(o_ref.dtype)


def _retention_core(hproj, gn_g, gn_b, batch, seq):
    hp = hproj.reshape(batch, seq, 6 * D_MODEL)
    log_gamma = jnp.log(1.0 - 2.0 ** (-5.0 - jnp.arange(RET_HEADS, dtype=F32)))
    qk_w = RET_HEADS_PER_STEP * RET_DK
    v_w = RET_HEADS_PER_STEP * RET_DV
    k_blk0 = D_MODEL // qk_w
    v_blk0 = 2 * D_MODEL // v_w
    g_blk0 = 4 * D_MODEL // v_w
    out = pl.pallas_call(
        _retention_kernel,
        out_shape=jax.ShapeDtypeStruct((batch, seq, 2 * D_MODEL), BF16),
        grid=(batch, RET_HEADS // RET_HEADS_PER_STEP, seq // RET_CHUNK),
        in_specs=[
            pl.BlockSpec(memory_space=pltpu.SMEM),
            pl.BlockSpec((1, RET_CHUNK, qk_w), lambda b, hg, c: (b, c, hg)),
            pl.BlockSpec((1, RET_CHUNK, qk_w), lambda b, hg, c: (b, c, k_blk0 + hg)),
            pl.BlockSpec((1, RET_CHUNK, v_w), lambda b, hg, c: (b, c, v_blk0 + hg)),
            pl.BlockSpec((1, RET_CHUNK, v_w), lambda b, hg, c: (b, c, g_blk0 + hg)),
            pl.BlockSpec((1, v_w), lambda b, hg, c: (0, hg)),
            pl.BlockSpec((1, v_w), lambda b, hg, c: (0, hg)),
        ],
        out_specs=pl.BlockSpec((1, RET_CHUNK, v_w), lambda b, hg, c: (b, c, hg)),
        scratch_shapes=[pltpu.VMEM((RET_HEADS_PER_STEP, RET_DK, RET_DV), F32)],
        compiler_params=_params("parallel", "parallel", "arbitrary"),
        name="retention_core",
    )(log_gamma, hp, hp, hp, hp, gn_g.reshape(1, 2 * D_MODEL), gn_b.reshape(1, 2 * D_MODEL))
    return out.reshape(batch * seq, 2 * D_MODEL)


def _swa_kernel(sink_ref, q_ref, kc_ref, vc_ref, kp_ref, vp_ref, o_ref):
    n = pl.program_id(1)
    blk = SWA_BLOCK
    lane = lax.broadcasted_iota(jnp.int32, (1, LANES), 1)
    low = lane < SWA_HD

    def both_halves(t, odd):
        t = t.astype(F32)
        keep = jnp.logical_not(low) if odd else low
        return jnp.where(keep, t, pltpu.roll(t, SWA_HD, axis=1)).astype(BF16)

    k_all = jnp.concatenate([kp_ref[0], kc_ref[0]], axis=0)
    v_all = jnp.concatenate([vp_ref[0], vc_ref[0]], axis=0)
    qi = lax.broadcasted_iota(jnp.int32, (blk, 2 * blk), 0)
    kj = lax.broadcasted_iota(jnp.int32, (blk, 2 * blk), 1)
    diff = qi + blk - kj
    first_key = jnp.where(n > 0, 0, blk)
    allowed = (diff >= 0) & (diff < SWA_WINDOW) & (kj >= first_key)

    for kv in range(SWA_KV_HEADS):
        kv_cols = slice((kv // 2) * LANES, (kv // 2 + 1) * LANES)
        k2 = both_halves(k_all[:, kv_cols], kv % 2)
        v2 = both_halves(v_all[:, kv_cols], kv % 2)
        outs = []
        for jh in range(SWA_GROUP):
            head = kv * SWA_GROUP + jh
            qp = q_ref[0, :, (head // 2) * LANES:(head // 2 + 1) * LANES].astype(F32)
            keep = jnp.logical_not(low) if head % 2 else low
            qh = jnp.where(keep, qp, 0.0).astype(BF16)
            sink = jnp.full((blk, 1), sink_ref[head] * LOG2_E, F32)
            s = lax.dot_general(qh, k2, (((1,), (1,)), ((), ())), preferred_element_type=F32)
            s = jnp.where(allowed, s, -jnp.inf)
            m = jnp.maximum(jnp.max(s, axis=-1, keepdims=True), sink)
            e = jnp.exp2(s - m)
            denom = jnp.sum(e, axis=-1, keepdims=True) + jnp.exp2(sink - m)
            ov = jnp.dot(e.astype(BF16), v2, preferred_element_type=F32)
            outs.append(ov / denom)
        for jp in range(SWA_GROUP // 2):
            c = kv * (SWA_GROUP // 2) + jp
            o_ref[0, :, c * LANES:(c + 1) * LANES] = jnp.where(low, outs[2 * jp], outs[2 * jp + 1]).astype(
                o_ref.dtype)


def _swa_core(hproj, sinks, batch, seq):
    width = hproj.shape[1]
    hp = hproj.reshape(batch, seq, width)
    q_width = SWA_HEADS * SWA_HD
    kv_width = SWA_KV_HEADS * SWA_HD
    k_blk = q_width // kv_width
    v_blk = k_blk + 1
    prev = lambda n: jnp.maximum(n - 1, 0)
    out = pl.pallas_call(
        _swa_kernel,
        out_shape=jax.ShapeDtypeStruct((batch, seq, q_width), BF16),
        grid=(batch, seq // SWA_BLOCK),
        in_specs=[
            pl.BlockSpec(memory_space=pltpu.SMEM),
            pl.BlockSpec((1, SWA_BLOCK, q_width), lambda b, n: (b, n, 0)),
            pl.BlockSpec((1, SWA_BLOCK, kv_width), lambda b, n: (b, n, k_blk)),
            pl.BlockSpec((1, SWA_BLOCK, kv_width), lambda b, n: (b, n, v_blk)),
            pl.BlockSpec((1, SWA_BLOCK, kv_width), lambda b, n: (b, prev(n), k_blk)),
            pl.BlockSpec((1, SWA_BLOCK, kv_width), lambda b, n: (b, prev(n), v_blk)),
        ],
        out_specs=pl.BlockSpec((1, SWA_BLOCK, q_width), lambda b, n: (b, n, 0)),
        compiler_params=_params("parallel", "parallel"),
        name="swa_core",
    )(sinks, hp, hp, hp, hp, hp)
    return out.reshape(batch * seq, q_width)


def _retention_mixer(h, ln, pos, w_in, gn_g, gn_b, w_out, batch, seq):
    inv_freq = ROPE_THETA ** (-jnp.linspace(0.0, 1.0, RET_DK // 2, dtype=F32))
    hproj = _proj_in(
        h, ln[0:1], pos, inv_freq.reshape(1, LANES), w_in, jnp.zeros((w_in.shape[1],), F32),
        col_tile=RET_COL_TILE, mode="pairs", q_cols=D_MODEL, k_cols=D_MODEL, q_scale=1.0,
        k_scale=RET_DK ** -0.5)
    gated = _retention_core(hproj, gn_g, gn_b, batch, seq)
    return _proj_out(gated, w_out, jnp.zeros((D_MODEL,), F32), ln[1:2], h)


def _swa_mixer(h, ln, pos, w_in, b_in, sinks, w_out, b_out, batch, seq):
    inv_freq = ROPE_THETA ** (-jnp.arange(0, SWA_HD, 2, dtype=F32) / SWA_HD)
    freq_row = jnp.tile(inv_freq, LANES // (SWA_HD // 2)).reshape(1, LANES)
    hproj = _proj_in(
        h, ln[0:1], pos, freq_row, w_in, b_in,
        col_tile=SWA_COL_TILE, mode="lane", q_cols=SWA_HEADS * SWA_HD, k_cols=SWA_KV_HEADS * SWA_HD,
        q_scale=LOG2_E * SWA_HD ** -0.5, k_scale=1.0)
    attn = _swa_core(hproj, sinks, batch, seq)
    return _proj_out(attn, w_out, b_out, ln[1:2], h)


def kernel(x, positions, ln_ffn1, ln_mix, ln_ffn2, ffn_w_in, ffn_w_out, pool_w, pool_scale, ret_w_in,
           ret_gn_g, ret_gn_b, ret_w_out, swa_w_in, swa_b_in, swa_sinks, swa_w_out, swa_b_out):
    batch, seq, _ = x.shape
    h = x.reshape(batch * seq, D_MODEL)
    pos = positions.reshape(batch * seq, 1)
    for i in range(DEPTH):
        h = _ffn(h, ln_ffn1[i], ffn_w_in, ffn_w_out, i, 0)
        kind, j = i % N_MIXERS, i // N_MIXERS
        if kind == 0:
            h = _pool_mixer(h, ln_mix[i], pool_w[j], pool_scale[j], batch, seq)
        elif kind == 1:
            h = _retention_mixer(h, ln_mix[i], pos, ret_w_in[j], ret_gn_g[j], ret_gn_b[j], ret_w_out[j],
                                 batch, seq)
        else:
            h = _swa_mixer(h, ln_mix[i], pos, swa_w_in[j], swa_b_in[j], swa_sinks[j], swa_w_out[j],
                           swa_b_out[j], batch, seq)
        h = _ffn(h, ln_ffn2[i], ffn_w_in, ffn_w_out, i, 1)
    return h.reshape(batch, seq, D_MODEL)
```

```python
import functools

import jax
import jax.numpy as jnp
from jax import lax
from jax.experimental import pallas as pl
from jax.experimental.pallas import tpu as pltpu

F32 = jnp.float32
BF16 = jnp.bfloat16

D_MODEL = 2048
DEPTH = 4
N_MIXERS = 3
EPS = 1e-6
D_FF = 5504
POOL_WINDOWS = (2, 4, 8, 16)
POOL_GROUP = D_MODEL // len(POOL_WINDOWS)
POOL_HALO = 16
RET_HEADS = 8
RET_DK = D_MODEL // RET_HEADS
RET_DV = 2 * D_MODEL // RET_HEADS
SWA_HEADS = 32
SWA_KV_HEADS = 4
SWA_HD = 64
SWA_GROUP = SWA_HEADS // SWA_KV_HEADS
SWA_WINDOW = 128
SWA_BLOCK = 128
ROPE_THETA = 10000.0
LOG2_E = 1.4426950408889634

LANES = 128
VMEM_LIMIT_BYTES = 58 * 1024 * 1024

ROW_TILE = 1024
NORM_ROWS = 128
FF_TILE = 256
PROJ_OUT_K = 1024
POOL_ROWS = 512
POOL_SUB = 128
RET_COL_TILE = 1024
SWA_COL_TILE = 512
RET_HEADS_PER_STEP = 8
RET_CHUNK = 256


def _rms(x, g):
    return x * lax.rsqrt(jnp.mean(x * x, axis=-1, keepdims=True) + EPS) * g


def _params(*sem):
    return pltpu.CompilerParams(dimension_semantics=sem, vmem_limit_bytes=VMEM_LIMIT_BYTES)


def _norm_rows_to(dst_ref, src_ref, g, rows, zero_ref=None):
    def body(i, carry):
        r = pl.ds(pl.multiple_of(i * NORM_ROWS, NORM_ROWS), NORM_ROWS)
        dst_ref[r, :] = _rms(src_ref[r, :], g).astype(dst_ref.dtype)
        if zero_ref is not None:
            zero_ref[r, :] = jnp.zeros((NORM_ROWS, zero_ref.shape[1]), zero_ref.dtype)
        return carry
    lax.fori_loop(0, rows // NORM_ROWS, body, 0)


def _residual_norm_rows(o_ref, h_ref, g, rows):
    def body(i, carry):
        r = pl.ds(pl.multiple_of(i * NORM_ROWS, NORM_ROWS), NORM_ROWS)
        o_ref[r, :] = h_ref[r, :] + _rms(o_ref[r, :], g)
        return carry
    lax.fori_loop(0, rows // NORM_ROWS, body, 0)


def _ff_offset(f, base=0):
    return LANES * (base // LANES + jnp.minimum(f * (FF_TILE // LANES), (D_FF - FF_TILE) // LANES))


def _ffn_kernel(h_ref, g1_ref, g2_ref, wg_ref, wu_ref, wo_ref, o_ref, xn_ref):
    f = pl.program_id(1)
    rows = h_ref.shape[0]

    @pl.when(f == 0)
    def _():
        _norm_rows_to(xn_ref, h_ref, g1_ref[...], rows, zero_ref=o_ref)

    xn = xn_ref[...]
    gate = jnp.dot(xn, wg_ref[...].astype(BF16), preferred_element_type=F32)
    up = jnp.dot(xn, wu_ref[...].astype(BF16), preferred_element_type=F32)
    repeated = f * FF_TILE - _ff_offset(f)
    fresh = lax.broadcasted_iota(jnp.int32, (1, FF_TILE), 1) >= repeated
    hid = jnp.where(fresh, jax.nn.silu(gate) * up, 0.0).astype(BF16)
    o_ref[...] += jnp.dot(hid, wo_ref[...].astype(BF16), preferred_element_type=F32)

    @pl.when(f == pl.num_programs(1) - 1)
    def _():
        _residual_norm_rows(o_ref, h_ref, 0.5 * g2_ref[...], rows)


def _ffn(h, ln, w_in_all, w_out_all, layer, half):
    m = h.shape[0]
    elem = pl.Element
    return pl.pallas_call(
        _ffn_kernel,
        out_shape=jax.ShapeDtypeStruct((m, D_MODEL), F32),
        grid=(m // ROW_TILE, pl.cdiv(D_FF, FF_TILE)),
        in_specs=[
            pl.BlockSpec((ROW_TILE, D_MODEL), lambda i, f: (i, 0)),
            pl.BlockSpec((1, D_MODEL), lambda i, f: (0, 0)),
            pl.BlockSpec((1, D_MODEL), lambda i, f: (0, 0)),
            pl.BlockSpec((None, None, elem(D_MODEL), elem(FF_TILE)),
                         lambda i, f: (layer, half, 0, _ff_offset(f))),
            pl.BlockSpec((None, None, elem(D_MODEL), elem(FF_TILE)),
                         lambda i, f: (layer, half, 0, _ff_offset(f, D_FF))),
            pl.BlockSpec((None, None, elem(FF_TILE), elem(D_MODEL)),
                         lambda i, f: (layer, half, _ff_offset(f), 0)),
        ],
        out_specs=pl.BlockSpec((ROW_TILE, D_MODEL), lambda i, f: (i, 0)),
        scratch_shapes=[pltpu.VMEM((ROW_TILE, D_MODEL), BF16)],
        compiler_params=_params("parallel", "arbitrary"),
        name="ffn",
    )(h, ln[0:1], ln[1:2], w_in_all, w_in_all, w_out_all)


def _pool_kernel(h_ref, halo_ref, gin_ref, gout_ref, band_ref, w_ref, scale_ref, o_ref):
    t = pl.program_id(1)
    rows = h_ref.shape[1]
    h = h_ref[0]
    gin = gin_ref[...]
    u = _rms(h, gin)
    has_prev = (t > 0).astype(F32)
    halo = _rms(halo_ref[0], gin) * has_prev
    hist = jnp.concatenate([halo, u], axis=0)
    hist_hi = hist.astype(BF16)
    hist_lo = (hist - hist_hi.astype(F32)).astype(BF16)
    seq_pos = t * rows + lax.broadcasted_iota(jnp.int32, (rows, 1), 0)
    outs = []
    for g, win in enumerate(POOL_WINDOWS):
        cols = slice(g * POOL_GROUP, (g + 1) * POOL_GROUP)
        band = band_ref[g]
        parts = []
        for sb in range(rows // POOL_SUB):
            hrows = slice(sb * POOL_SUB, (sb + 1) * POOL_SUB + POOL_HALO)
            parts.append(jnp.dot(band, hist_hi[hrows, cols], preferred_element_type=F32)
                         + jnp.dot(band, hist_lo[hrows, cols], preferred_element_type=F32))
        others = jnp.concatenate(parts, axis=0)
        count = jnp.minimum(seq_pos + 1, win).astype(F32)
        own = u[:, cols]
        mixed = ((others + own) / count - own).astype(BF16)
        y = jnp.dot(mixed, w_ref[g], preferred_element_type=F32)
        outs.append(y * scale_ref[:, cols])
    y = jnp.concatenate(outs, axis=1)
    o_ref[0] = h + _rms(y, gout_ref[...])


def _pool_mixer(h, ln, w_group, scale, batch, seq):
    h3 = h.reshape(batch, seq, D_MODEL)
    halo_blocks = POOL_ROWS // POOL_HALO
    back = (jnp.arange(POOL_SUB)[:, None] + POOL_HALO) - jnp.arange(POOL_SUB + POOL_HALO)[None, :]
    band = jnp.stack([(back >= 1) & (back < win) for win in POOL_WINDOWS]).astype(BF16)
    out = pl.pallas_call(
        _pool_kernel,
        out_shape=jax.ShapeDtypeStruct(h3.shape, F32),
        grid=(batch, seq // POOL_ROWS),
        in_specs=[
            pl.BlockSpec((1, POOL_ROWS, D_MODEL), lambda b, t: (b, t, 0)),
            pl.BlockSpec((1, POOL_HALO, D_MODEL),
                         lambda b, t: (b, jnp.maximum(t * halo_blocks - 1, 0), 0)),
            pl.BlockSpec((1, D_MODEL), lambda b, t: (0, 0)),
            pl.BlockSpec((1, D_MODEL), lambda b, t: (0, 0)),
            pl.BlockSpec(band.shape, lambda b, t: (0, 0, 0)),
            pl.BlockSpec((len(POOL_WINDOWS), POOL_GROUP, POOL_GROUP), lambda b, t: (0, 0, 0)),
            pl.BlockSpec((1, D_MODEL), lambda b, t: (0, 0)),
        ],
        out_specs=pl.BlockSpec((1, POOL_ROWS, D_MODEL), lambda b, t: (b, t, 0)),
        compiler_params=_params("parallel", "parallel"),
        name="pool_mixer",
    )(h3, h3, ln[0:1], ln[1:2], band, w_group.astype(BF16), scale.reshape(1, D_MODEL))
    return out.reshape(batch * seq, D_MODEL)


def _proj_in_kernel(h_ref, g_ref, pos_ref, freq_ref, w_ref, b_ref, o_ref, xn_ref, rot_a_ref, rot_b_ref,
                    *, mode, q_cols, k_cols, q_scale, k_scale):
    j = pl.program_id(1)
    rows, col_tile = o_ref.shape
    lane = lax.broadcasted_iota(jnp.int32, (1, LANES), 1)
    first_half = (lane % SWA_HD) < (SWA_HD // 2)

    @pl.when(j == 0)
    def _():
        _norm_rows_to(xn_ref, h_ref, g_ref[...], rows)
        ang = pos_ref[...].astype(F32) * freq_ref[...]
        cos = jnp.cos(ang)
        sin = jnp.sin(ang)
        if mode == "lane":
            sin = jnp.where(first_half, -sin, sin)
        rot_a_ref[0] = cos * q_scale
        rot_b_ref[0] = sin * q_scale
        rot_a_ref[1] = cos * k_scale
        rot_b_ref[1] = sin * k_scale
        rot_a_ref[2] = jnp.ones_like(cos)
        rot_b_ref[2] = jnp.zeros_like(sin)

    y = jnp.dot(xn_ref[...], w_ref[...].astype(BF16), preferred_element_type=F32) + b_ref[...]
    unit = 2 * LANES if mode == "pairs" else LANES
    outs = []
    for c in range(col_tile // unit):
        col = j * col_tile + c * unit
        kind = jnp.where(col < q_cols, 0, jnp.where(col < q_cols + k_cols, 1, 2))
        a = rot_a_ref[kind]
        b = rot_b_ref[kind]
        if mode == "pairs":
            x1 = y[:, c * unit:c * unit + LANES]
            x2 = y[:, c * unit + LANES:(c + 1) * unit]
            outs += [x1 * a - x2 * b, x2 * a + x1 * b]
        else:
            half = SWA_HD // 2
            xb = y[:, c * unit:(c + 1) * unit]
            partner = jnp.where(first_half, pltpu.roll(xb, LANES - half, axis=1), pltpu.roll(xb, half, axis=1))
            outs.append(xb * a + partner * b)
    o_ref[...] = jnp.concatenate(outs, axis=1).astype(o_ref.dtype)


def _proj_in(h, g, pos, freq_row, w, bias, *, col_tile, mode, q_cols, k_cols, q_scale, k_scale):
    m = h.shape[0]
    n = w.shape[1]
    kern = functools.partial(_proj_in_kernel, mode=mode, q_cols=q_cols, k_cols=k_cols,
                             q_scale=q_scale, k_scale=k_scale)
    return pl.pallas_call(
        kern,
        out_shape=jax.ShapeDtypeStruct((m, n), BF16),
        grid=(m // ROW_TILE, n // col_tile),
        in_specs=[
            pl.BlockSpec((ROW_TILE, D_MODEL), lambda i, j: (i, 0)),
            pl.BlockSpec((1, D_MODEL), lambda i, j: (0, 0)),
            pl.BlockSpec((ROW_TILE, 1), lambda i, j: (i, 0)),
            pl.BlockSpec((1, LANES), lambda i, j: (0, 0)),
            pl.BlockSpec((D_MODEL, col_tile), lambda i, j: (0, j)),
            pl.BlockSpec((1, col_tile), lambda i, j: (0, j)),
        ],
        out_specs=pl.BlockSpec((ROW_TILE, col_tile), lambda i, j: (i, j)),
        scratch_shapes=[pltpu.VMEM((ROW_TILE, D_MODEL), BF16),
                        pltpu.VMEM((3, ROW_TILE, LANES), F32),
                        pltpu.VMEM((3, ROW_TILE, LANES), F32)],
        compiler_params=_params("parallel", "arbitrary"),
        name="proj_in_" + mode,
    )(h, g, pos, freq_row, w, bias.reshape(1, n))


def _proj_out_kernel(x_ref, w_ref, b_ref, g_ref, h_ref, o_ref):
    k = pl.program_id(1)
    rows = h_ref.shape[0]

    def partial_product():
        return jnp.dot(x_ref[...], w_ref[...].astype(BF16), preferred_element_type=F32)

    @pl.when(k == 0)
    def _():
        o_ref[...] = partial_product() + b_ref[...]

    @pl.when(k > 0)
    def _():
        o_ref[...] += partial_product()

    @pl.when(k == pl.num_programs(1) - 1)
    def _():
        _residual_norm_rows(o_ref, h_ref, g_ref[...], rows)


def _proj_out(x, w, bias, g, h):
    m, kdim = x.shape
    return pl.pallas_call(
        _proj_out_kernel,
        out_shape=jax.ShapeDtypeStruct((m, D_MODEL), F32),
        grid=(m // ROW_TILE, kdim // PROJ_OUT_K),
        in_specs=[
            pl.BlockSpec((ROW_TILE, PROJ_OUT_K), lambda i, k: (i, k)),
            pl.BlockSpec((PROJ_OUT_K, D_MODEL), lambda i, k: (k, 0)),
            pl.BlockSpec((1, D_MODEL), lambda i, k: (0, 0)),
            pl.BlockSpec((1, D_MODEL), lambda i, k: (0, 0)),
            pl.BlockSpec((ROW_TILE, D_MODEL), lambda i, k: (i, 0)),
        ],
        out_specs=pl.BlockSpec((ROW_TILE, D_MODEL), lambda i, k: (i, 0)),
        compiler_params=_params("parallel", "arbitrary"),
        name="proj_out",
    )(x, w, bias.reshape(1, D_MODEL), g, h)


def _retention_kernel(lg_ref, q_ref, k_ref, v_ref, gate_ref, gng_ref, gnb_ref, o_ref, state_ref):
    group = pl.program_id(1)
    chunk = q_ref.shape[1]

    @pl.when(pl.program_id(2) == 0)
    def _():
        state_ref[...] = jnp.zeros_like(state_ref)

    qi = lax.broadcasted_iota(jnp.int32, (chunk, chunk), 0)
    kj = lax.broadcasted_iota(jnp.int32, (chunk, chunk), 1)
    rel = (qi - kj).astype(F32)
    idx = lax.broadcasted_iota(jnp.int32, (chunk, 1), 0).astype(F32)

    for t in range(RET_HEADS_PER_STEP):
        log_gamma = jnp.full((1, 1), lg_ref[group * RET_HEADS_PER_STEP + t], F32)
        qk_cols = slice(t * RET_DK, (t + 1) * RET_DK)
        v_cols = slice(t * RET_DV, (t + 1) * RET_DV)
        q = q_ref[0, :, qk_cols]
        k = k_ref[0, :, qk_cols]
        v = v_ref[0, :, v_cols]
        inner_decay = jnp.where(rel >= 0, jnp.exp(jnp.maximum(rel, 0.0) * log_gamma), 0.0)
        cross_decay = jnp.exp((idx + 1.0) * log_gamma)
        state_decay = jnp.exp((chunk - 1.0 - idx) * log_gamma)
        chunk_decay = jnp.exp(float(chunk) * log_gamma)

        scores = lax.dot_general(q, k, (((1,), (1,)), ((), ())), preferred_element_type=F32) * inner_decay
        inner = jnp.dot(scores.astype(BF16), v, preferred_element_type=F32)
        state = state_ref[t]
        cross = jnp.dot(q, state.astype(BF16), preferred_element_type=F32) * cross_decay
        k_decayed = (k.astype(F32) * state_decay).astype(BF16)
        state_ref[t] = state * chunk_decay + lax.dot_general(
            k_decayed, v, (((0,), (0,)), ((), ())), preferred_element_type=F32)

        o = inner + cross
        mu = jnp.mean(o, axis=-1, keepdims=True)
        var = jnp.mean(jnp.square(o - mu), axis=-1, keepdims=True)
        o = (o - mu) * lax.rsqrt(var + EPS)
        o = o * gng_ref[:, v_cols] + gnb_ref[:, v_cols]
        o_ref[0, :, v_cols] = (jax.nn.silu(gate_ref[0, :, v_cols].astype(F32)) * o).astype(o_ref.dtype)


def _retention_core(hproj, gn_g, gn_b, batch, seq):
    hp = hproj.reshape(batch, seq, 6 * D_MODEL)
    log_gamma = jnp.log(1.0 - 2.0 ** (-5.0 - jnp.arange(RET_HEADS, dtype=F32)))
    qk_w = RET_HEADS_PER_STEP * RET_DK
    v_w = RET_HEADS_PER_STEP * RET_DV
    k_blk0 = D_MODEL // qk_w
    v_blk0 = 2 * D_MODEL // v_w
    g_blk0 = 4 * D_MODEL // v_w
    out = pl.pallas_call(
        _retention_kernel,
        out_shape=jax.ShapeDtypeStruct((batch, seq, 2 * D_MODEL), BF16),
        grid=(batch, RET_HEADS // RET_HEADS_PER_STEP, seq // RET_CHUNK),
        in_specs=[
            pl.BlockSpec(memory_space=pltpu.SMEM),
            pl.BlockSpec((1, RET_CHUNK, qk_w), lambda b, hg, c: (b, c, hg)),
            pl.BlockSpec((1, RET_CHUNK, qk_w), lambda b, hg, c: (b, c, k_blk0 + hg)),
            pl.BlockSpec((1, RET_CHUNK, v_w), lambda b, hg, c: (b, c, v_blk0 + hg)),
            pl.BlockSpec((1, RET_CHUNK, v_w), lambda b, hg, c: (b, c, g_blk0 + hg)),
            pl.BlockSpec((1, v_w), lambda b, hg, c: (0, hg)),
            pl.BlockSpec((1, v_w), lambda b, hg, c: (0, hg)),
        ],
        out_specs=pl.BlockSpec((1, RET_CHUNK, v_w), lambda b, hg, c: (b, c, hg)),
        scratch_shapes=[pltpu.VMEM((RET_HEADS_PER_STEP, RET_DK, RET_DV), F32)],
        compiler_params=_params("parallel", "parallel", "arbitrary"),
        name="retention_core",
    )(log_gamma, hp, hp, hp, hp, gn_g.reshape(1, 2 * D_MODEL), gn_b.reshape(1, 2 * D_MODEL))
    return out.reshape(batch * seq, 2 * D_MODEL)


def _swa_kernel(sink_ref, q_ref, kc_ref, vc_ref, kp_ref, vp_ref, o_ref):
    n = pl.program_id(1)
    blk = SWA_BLOCK
    lane = lax.broadcasted_iota(jnp.int32, (1, LANES), 1)
    low = lane < SWA_HD

    def both_halves(t, odd):
        t = t.astype(F32)
        keep = jnp.logical_not(low) if odd else low
        return jnp.where(keep, t, pltpu.roll(t, SWA_HD, axis=1)).astype(BF16)

    k_all = jnp.concatenate([kp_ref[0], kc_ref[0]], axis=0)
    v_all = jnp.concatenate([vp_ref[0], vc_ref[0]], axis=0)
    qi = lax.broadcasted_iota(jnp.int32, (blk, 2 * blk), 0)
    kj = lax.broadcasted_iota(jnp.int32, (blk, 2 * blk), 1)
    diff = qi + blk - kj
    first_key = jnp.where(n > 0, 0, blk)
    allowed = (diff >= 0) & (diff < SWA_WINDOW) & (kj >= first_key)

    for kv in range(SWA_KV_HEADS):
        kv_cols = slice((kv // 2) * LANES, (kv // 2 + 1) * LANES)
        k2 = both_halves(k_all[:, kv_cols], kv % 2)
        v2 = both_halves(v_all[:, kv_cols], kv % 2)
        outs = []
        for jh in range(SWA_GROUP):
            head = kv * SWA_GROUP + jh
            qp = q_ref[0, :, (head // 2) * LANES:(head // 2 + 1) * LANES].astype(F32)
            keep = jnp.logical_not(low) if head % 2 else low
            qh = jnp.where(keep, qp, 0.0).astype(BF16)
            sink = jnp.full((blk, 1), sink_ref[head] * LOG2_E, F32)
            s = lax.dot_general(qh, k2, (((1,), (1,)), ((), ())), preferred_element_type=F32)
            s = jnp.where(allowed, s, -jnp.inf)
            m = jnp.maximum(jnp.max(s, axis=-1, keepdims=True), sink)
            e = jnp.exp2(s - m)
            denom = jnp.sum(e, axis=-1, keepdims=True) + jnp.exp2(sink - m)
            ov = jnp.dot(e.astype(BF16), v2, preferred_element_type=F32)
            outs.append(ov / denom)
        for jp in range(SWA_GROUP // 2):
            c = kv * (SWA_GROUP // 2) + jp
            o_ref[0, :, c * LANES:(c + 1) * LANES] = jnp.where(low, outs[2 * jp], outs[2 * jp + 1]).astype(
                o_ref.dtype)


def _swa_core(hproj, sinks, batch, seq):
    width = hproj.shape[1]
    hp = hproj.reshape(batch, seq, width)
    q_width = SWA_HEADS * SWA_HD
    kv_width = SWA_KV_HEADS * SWA_HD
    k_blk = q_width // kv_width
    v_blk = k_blk + 1
    prev = lambda n: jnp.maximum(n - 1, 0)
    out = pl.pallas_call(
        _swa_kernel,
        out_shape=jax.ShapeDtypeStruct((batch, seq, q_width), BF16),
        grid=(batch, seq // SWA_BLOCK),
        in_specs=[
            pl.BlockSpec(memory_space=pltpu.SMEM),
            pl.BlockSpec((1, SWA_BLOCK, q_width), lambda b, n: (b, n, 0)),
            pl.BlockSpec((1, SWA_BLOCK, kv_width), lambda b, n: (b, n, k_blk)),
            pl.BlockSpec((1, SWA_BLOCK, kv_width), lambda b, n: (b, n, v_blk)),
            pl.BlockSpec((1, SWA_BLOCK, kv_width), lambda b, n: (b, prev(n), k_blk)),
            pl.BlockSpec((1, SWA_BLOCK, kv_width), lambda b, n: (b, prev(n), v_blk)),
        ],
        out_specs=pl.BlockSpec((1, SWA_BLOCK, q_width), lambda b, n: (b, n, 0)),
        compiler_params=_params("parallel", "parallel"),
        name="swa_core",
    )(sinks, hp, hp, hp, hp, hp)
    return out.reshape(batch * seq, q_width)


def _retention_mixer(h, ln, pos, w_in, gn_g, gn_b, w_out, batch, seq):
    inv_freq = ROPE_THETA ** (-jnp.linspace(0.0, 1.0, RET_DK // 2, dtype=F32))
    hproj = _proj_in(
        h, ln[0:1], pos, inv_freq.reshape(1, LANES), w_in, jnp.zeros((w_in.shape[1],), F32),
        col_tile=RET_COL_TILE, mode="pairs", q_cols=D_MODEL, k_cols=D_MODEL, q_scale=1.0,
        k_scale=RET_DK ** -0.5)
    gated = _retention_core(hproj, gn_g, gn_b, batch, seq)
    return _proj_out(gated, w_out, jnp.zeros((D_MODEL,), F32), ln[1:2], h)


def _swa_mixer(h, ln, pos, w_in, b_in, sinks, w_out, b_out, batch, seq):
    inv_freq = ROPE_THETA ** (-jnp.arange(0, SWA_HD, 2, dtype=F32) / SWA_HD)
    freq_row = jnp.tile(inv_freq, LANES // (SWA_HD // 2)).reshape(1, LANES)
    hproj = _proj_in(
        h, ln[0:1], pos, freq_row, w_in, b_in,
        col_tile=SWA_COL_TILE, mode="lane", q_cols=SWA_HEADS * SWA_HD, k_cols=SWA_KV_HEADS * SWA_HD,
        q_scale=LOG2_E * SWA_HD ** -0.5, k_scale=1.0)
    attn = _swa_core(hproj, sinks, batch, seq)
    return _proj_out(attn, w_out, b_out, ln[1:2], h)


def kernel(x, positions, ln_ffn1, ln_mix, ln_ffn2, ffn_w_in, ffn_w_out, pool_w, pool_scale, ret_w_in,
           ret_gn_g, ret_gn_b, ret_w_out, swa_w_in, swa_b_in, swa_sinks, swa_w_out, swa_b_out):
    batch, seq, _ = x.shape
    h = x.reshape(batch * seq, D_MODEL)
    pos = positions.reshape(batch * seq, 1)
    for i in range(DEPTH):
        h = _ffn(h, ln_ffn1[i], ffn_w_in, ffn_w_out, i, 0)
        kind, j = i % N_MIXERS, i // N_MIXERS
        if kind == 0:
            h = _pool_mixer(h, ln_mix[i], pool_w[j], pool_scale[j], batch, seq)
        elif kind == 1:
            h = _retention_mixer(h, ln_mix[i], pos, ret_w_in[j], ret_gn_g[j], ret_gn_b[j], ret_w_out[j],
                                 batch, seq)
        else:
            h = _swa_mixer(h, ln_mix[i], pos, swa_w_in[j], swa_b_in[j], swa_sinks[j], swa_w_out[j],
                           swa_b_out[j], batch, seq)
        h = _ffn(h, ln_ffn2[i], ffn_w_in, ffn_w_out, i, 1)
    return h.reshape(batch, seq, D_MODEL)
```

```python
import functools

import jax
import jax.numpy as jnp
from jax import lax
from jax.experimental import pallas as pl
from jax.experimental.pallas import tpu as pltpu

F32 = jnp.float32
BF16 = jnp.bfloat16

D_MODEL = 2048
DEPTH = 4
N_MIXERS = 3
EPS = 1e-6
D_FF = 5504
POOL_WINDOWS = (2, 4, 8, 16)
POOL_GROUP = D_MODEL // len(POOL_WINDOWS)
POOL_HALO = 16
RET_HEADS = 8
RET_DK = D_MODEL // RET_HEADS
RET_DV = 2 * D_MODEL // RET_HEADS
SWA_HEADS = 32
SWA_KV_HEADS = 4
SWA_HD = 64
SWA_GROUP = SWA_HEADS // SWA_KV_HEADS
SWA_WINDOW = 128
SWA_BLOCK = 128
ROPE_THETA = 10000.0
LOG2_E = 1.4426950408889634

LANES = 128
VMEM_LIMIT_BYTES = 58 * 1024 * 1024

ROW_TILE = 1024
NORM_ROWS = 128
FF_TILE = 512
PROJ_OUT_K = 1024
POOL_ROWS = 512
POOL_SUB = 128
RET_COL_TILE = 1024
SWA_COL_TILE = 512
RET_HEADS_PER_STEP = 8
RET_CHUNK = 256


def _rms(x, g):
    return x * lax.rsqrt(jnp.mean(x * x, axis=-1, keepdims=True) + EPS) * g


def _params(*sem):
    return pltpu.CompilerParams(dimension_semantics=sem, vmem_limit_bytes=VMEM_LIMIT_BYTES)


def _norm_rows_to(dst_ref, src_ref, g, rows):
    def body(i, carry):
        r = pl.ds(pl.multiple_of(i * NORM_ROWS, NORM_ROWS), NORM_ROWS)
        dst_ref[r, :] = _rms(src_ref[r, :], g).astype(dst_ref.dtype)
        return carry
    lax.fori_loop(0, rows // NORM_ROWS, body, 0)


def _residual_norm_rows(o_ref, h_ref, g, rows):
    def body(i, carry):
        r = pl.ds(pl.multiple_of(i * NORM_ROWS, NORM_ROWS), NORM_ROWS)
        o_ref[r, :] = h_ref[r, :] + _rms(o_ref[r, :], g)
        return carry
    lax.fori_loop(0, rows // NORM_ROWS, body, 0)


def _ff_offset(f, base=0):
    return LANES * (base // LANES + jnp.minimum(f * (FF_TILE // LANES), (D_FF - FF_TILE) // LANES))


def _ffn_kernel(h_hbm, g1_ref, g2_ref, wg_ref, wu_ref, wo_ref, o_ref, xn_ref, hbuf_ref, hsem):
    i = pl.program_id(0)
    f = pl.program_id(1)
    last = pl.num_programs(1) - 1
    rows = o_ref.shape[0]
    pieces = rows // NORM_ROWS

    def h_copy(tile, p):
        src = h_hbm.at[pl.ds(tile * rows + p * NORM_ROWS, NORM_ROWS), :]
        return pltpu.make_async_copy(src, hbuf_ref.at[p % 2], hsem.at[p % 2])

    def stream_h(use_piece):
        for p in range(pieces):
            if p + 1 < pieces:
                h_copy(i, p + 1).start()
            h_copy(i, p).wait()
            use_piece(slice(p * NORM_ROWS, (p + 1) * NORM_ROWS), hbuf_ref[p % 2])

    @pl.when(f == 0)
    def _():
        @pl.when(i == 0)
        def _():
            h_copy(i, 0).start()

        g1 = g1_ref[...]

        def to_xn(r, hp):
            xn_ref[r, :] = _rms(hp, g1).astype(xn_ref.dtype)
            o_ref[r, :] = jnp.zeros((NORM_ROWS, o_ref.shape[1]), o_ref.dtype)
        stream_h(to_xn)

    @pl.when(f == last)
    def _():
        h_copy(i, 0).start()

    xn = xn_ref[...]
    gate = jnp.dot(xn, wg_ref[...].astype(BF16), preferred_element_type=F32)
    up = jnp.dot(xn, wu_ref[...].astype(BF16), preferred_element_type=F32)
    repeated = f * FF_TILE - _ff_offset(f)
    fresh = lax.broadcasted_iota(jnp.int32, (1, FF_TILE), 1) >= repeated
    hid = jnp.where(fresh, jax.nn.silu(gate) * up, 0.0).astype(BF16)
    o_ref[...] += jnp.dot(hid, wo_ref[...].astype(BF16), preferred_element_type=F32)

    @pl.when(f == last)
    def _():
        g2 = 0.5 * g2_ref[...]

        def residual(r, hp):
            o_ref[r, :] = hp + _rms(o_ref[r, :], g2)
        stream_h(residual)

        @pl.when(i + 1 < pl.num_programs(0))
        def _():
            h_copy(i + 1, 0).start()


def _ffn(h, ln, w_in_all, w_out_all, layer, half):
    m = h.shape[0]
    n_ff = pl.cdiv(D_FF, FF_TILE)
    assert n_ff > 1 and ROW_TILE % (2 * NORM_ROWS) == 0
    elem = pl.Element
    return pl.pallas_call(
        _ffn_kernel,
        out_shape=jax.ShapeDtypeStruct((m, D_MODEL), F32),
        grid=(m // ROW_TILE, n_ff),
        in_specs=[
            pl.BlockSpec(memory_space=pl.ANY),
            pl.BlockSpec((1, D_MODEL), lambda i, f: (0, 0)),
            pl.BlockSpec((1, D_MODEL), lambda i, f: (0, 0)),
            pl.BlockSpec((None, None, elem(D_MODEL), elem(FF_TILE)),
                         lambda i, f: (layer, half, 0, _ff_offset(f))),
            pl.BlockSpec((None, None, elem(D_MODEL), elem(FF_TILE)),
                         lambda i, f: (layer, half, 0, _ff_offset(f, D_FF))),
            pl.BlockSpec((None, None, elem(FF_TILE), elem(D_MODEL)),
                         lambda i, f: (layer, half, _ff_offset(f), 0)),
        ],
        out_specs=pl.BlockSpec((ROW_TILE, D_MODEL), lambda i, f: (i, 0)),
        scratch_shapes=[pltpu.VMEM((ROW_TILE, D_MODEL), BF16),
                        pltpu.VMEM((2, NORM_ROWS, D_MODEL), F32),
                        pltpu.SemaphoreType.DMA((2,))],
        compiler_params=_params("arbitrary", "arbitrary"),
        name="ffn",
    )(h, ln[0:1], ln[1:2], w_in_all, w_in_all, w_out_all)


def _pool_kernel(h_ref, halo_ref, gin_ref, gout_ref, band_ref, w_ref, scale_ref, o_ref):
    t = pl.program_id(1)
    rows = h_ref.shape[1]
    h = h_ref[0]
    gin = gin_ref[...]
    u = _rms(h, gin)
    has_prev = (t > 0).astype(F32)
    halo = _rms(halo_ref[0], gin) * has_prev
    lead = jnp.zeros((POOL_SUB - POOL_HALO, halo.shape[1]), F32)
    hist = jnp.concatenate([lead, halo, u], axis=0)
    hist_hi = hist.astype(BF16)
    hist_lo = (hist - hist_hi.astype(F32)).astype(BF16)
    seq_pos = t * rows + lax.broadcasted_iota(jnp.int32, (rows, 1), 0)
    outs = []
    for g, win in enumerate(POOL_WINDOWS):
        cols = slice(g * POOL_GROUP, (g + 1) * POOL_GROUP)
        band = band_ref[g]
        parts = []
        for sb in range(rows // POOL_SUB):
            hrows = slice(sb * POOL_SUB, (sb + 2) * POOL_SUB)
            parts.append(jnp.dot(band, hist_hi[hrows, cols], preferred_element_type=F32)
                         + jnp.dot(band, hist_lo[hrows, cols], preferred_element_type=F32))
        others = jnp.concatenate(parts, axis=0)
        count = jnp.minimum(seq_pos + 1, win).astype(F32)
        own = u[:, cols]
        mixed = ((others + own) / count - own).astype(BF16)
        y = jnp.dot(mixed, w_ref[g], preferred_element_type=F32)
        outs.append(y * scale_ref[:, cols])
    y = jnp.concatenate(outs, axis=1)
    o_ref[0] = h + _rms(y, gout_ref[...])


def _pool_mixer(h, ln, w_group, scale, batch, seq):
    h3 = h.reshape(batch, seq, D_MODEL)
    halo_blocks = POOL_ROWS // POOL_HALO
    back = (jnp.arange(POOL_SUB)[:, None] + POOL_SUB) - jnp.arange(2 * POOL_SUB)[None, :]
    band = jnp.stack([(back >= 1) & (back < win) for win in POOL_WINDOWS]).astype(BF16)
    out = pl.pallas_call(
        _pool_kernel,
        out_shape=jax.ShapeDtypeStruct(h3.shape, F32),
        grid=(batch, seq // POOL_ROWS),
        in_specs=[
            pl.BlockSpec((1, POOL_ROWS, D_MODEL), lambda b, t: (b, t, 0)),
            pl.BlockSpec((1, POOL_HALO, D_MODEL),
                         lambda b, t: (b, jnp.maximum(t * halo_blocks - 1, 0), 0)),
            pl.BlockSpec((1, D_MODEL), lambda b, t: (0, 0)),
            pl.BlockSpec((1, D_MODEL), lambda b, t: (0, 0)),
            pl.BlockSpec(band.shape, lambda b, t: (0, 0, 0)),
            pl.BlockSpec((len(POOL_WINDOWS), POOL_GROUP, POOL_GROUP), lambda b, t: (0, 0, 0)),
            pl.BlockSpec((1, D_MODEL), lambda b, t: (0, 0)),
        ],
        out_specs=pl.BlockSpec((1, POOL_ROWS, D_MODEL), lambda b, t: (b, t, 0)),
        compiler_params=_params("parallel", "parallel"),
        name="pool_mixer",
    )(h3, h3, ln[0:1], ln[1:2], band, w_group.astype(BF16), scale.reshape(1, D_MODEL))
    return out.reshape(batch * seq, D_MODEL)


def _proj_in_kernel(h_ref, g_ref, pos_ref, freq_ref, w_ref, b_ref, o_ref, xn_ref, rot_a_ref, rot_b_ref,
                    *, mode, q_cols, k_cols, q_scale, k_scale):
    j = pl.program_id(1)
    rows, col_tile = o_ref.shape
    lane = lax.broadcasted_iota(jnp.int32, (1, LANES), 1)
    first_half = (lane % SWA_HD) < (SWA_HD // 2)

    @pl.when(j == 0)
    def _():
        _norm_rows_to(xn_ref, h_ref, g_ref[...], rows)
        ang = pos_ref[...].astype(F32) * freq_ref[...]
        cos = jnp.cos(ang)
        sin = jnp.sin(ang)
        if mode == "lane":
            sin = jnp.where(first_half, -sin, sin)
        rot_a_ref[0] = cos * q_scale
        rot_b_ref[0] = sin * q_scale
        rot_a_ref[1] = cos * k_scale
        rot_b_ref[1] = sin * k_scale
        rot_a_ref[2] = jnp.ones_like(cos)
        rot_b_ref[2] = jnp.zeros_like(sin)

    y = jnp.dot(xn_ref[...], w_ref[...].astype(BF16), preferred_element_type=F32) + b_ref[...]
    unit = 2 * LANES if mode == "pairs" else LANES
    outs = []
    for c in range(col_tile // unit):
        col = j * col_tile + c * unit
        kind = jnp.where(col < q_cols, 0, jnp.where(col < q_cols + k_cols, 1, 2))
        a = rot_a_ref[kind]
        b = rot_b_ref[kind]
        if mode == "pairs":
            x1 = y[:, c * unit:c * unit + LANES]
            x2 = y[:, c * unit + LANES:(c + 1) * unit]
            outs += [x1 * a - x2 * b, x2 * a + x1 * b]
        else:
            half = SWA_HD // 2
            xb = y[:, c * unit:(c + 1) * unit]
            partner = jnp.where(first_half, pltpu.roll(xb, LANES - half, axis=1), pltpu.roll(xb, half, axis=1))
            outs.append(xb * a + partner * b)
    o_ref[...] = jnp.concatenate(outs, axis=1).astype(o_ref.dtype)


def _proj_in(h, g, pos, freq_row, w, bias, *, col_tile, mode, q_cols, k_cols, q_scale, k_scale):
    m = h.shape[0]
    n = w.shape[1]
    kern = functools.partial(_proj_in_kernel, mode=mode, q_cols=q_cols, k_cols=k_cols,
                             q_scale=q_scale, k_scale=k_scale)
    return pl.pallas_call(
        kern,
        out_shape=jax.ShapeDtypeStruct((m, n), BF16),
        grid=(m // ROW_TILE, n // col_tile),
        in_specs=[
            pl.BlockSpec((ROW_TILE, D_MODEL), lambda i, j: (i, 0)),
            pl.BlockSpec((1, D_MODEL), lambda i, j: (0, 0)),
            pl.BlockSpec((ROW_TILE, 1), lambda i, j: (i, 0)),
            pl.BlockSpec((1, LANES), lambda i, j: (0, 0)),
            pl.BlockSpec((D_MODEL, col_tile), lambda i, j: (0, j)),
            pl.BlockSpec((1, col_tile), lambda i, j: (0, j)),
        ],
        out_specs=pl.BlockSpec((ROW_TILE, col_tile), lambda i, j: (i, j)),
        scratch_shapes=[pltpu.VMEM((ROW_TILE, D_MODEL), BF16),
                        pltpu.VMEM((3, ROW_TILE, LANES), F32),
                        pltpu.VMEM((3, ROW_TILE, LANES), F32)],
        compiler_params=_params("parallel", "arbitrary"),
        name="proj_in_" + mode,
    )(h, g, pos, freq_row, w, bias.reshape(1, n))


def _proj_out_kernel(x_ref, w_ref, b_ref, g_ref, h_ref, o_ref):
    k = pl.program_id(1)
    rows = h_ref.shape[0]

    def partial_product():
        return jnp.dot(x_ref[...], w_ref[...].astype(BF16), preferred_element_type=F32)

    @pl.when(k == 0)
    def _():
        o_ref[...] = partial_product() + b_ref[...]

    @pl.when(k > 0)
    def _():
        o_ref[...] += partial_product()

    @pl.when(k == pl.num_programs(1) - 1)
    def _():
        _residual_norm_rows(o_ref, h_ref, g_ref[...], rows)


def _proj_out(x, w, bias, g, h):
    m, kdim = x.shape
    return pl.pallas_call(
        _proj_out_kernel,
        out_shape=jax.ShapeDtypeStruct((m, D_MODEL), F32),
        grid=(m // ROW_TILE, kdim // PROJ_OUT_K),
        in_specs=[
            pl.BlockSpec((ROW_TILE, PROJ_OUT_K), lambda i, k: (i, k)),
            pl.BlockSpec((PROJ_OUT_K, D_MODEL), lambda i, k: (k, 0)),
            pl.BlockSpec((1, D_MODEL), lambda i, k: (0, 0)),
            pl.BlockSpec((1, D_MODEL), lambda i, k: (0, 0)),
            pl.BlockSpec((ROW_TILE, D_MODEL), lambda i, k: (i, 0)),
        ],
        out_specs=pl.BlockSpec((ROW_TILE, D_MODEL), lambda i, k: (i, 0)),
        compiler_params=_params("parallel", "arbitrary"),
        name="proj_out",
    )(x, w, bias.reshape(1, D_MODEL), g, h)


def _retention_kernel(lg_ref, q_ref, k_ref, v_ref, gate_ref, gng_ref, gnb_ref, o_ref, state_ref):
    group = pl.program_id(1)
    chunk = q_ref.shape[1]

    @pl.when(pl.program_id(2) == 0)
    def _():
        state_ref[...] = jnp.zeros_like(state_ref)

    qi = lax.broadcasted_iota(jnp.int32, (chunk, chunk), 0)
    kj = lax.broadcasted_iota(jnp.int32, (chunk, chunk), 1)
    rel = (qi - kj).astype(F32)
    idx = lax.broadcasted_iota(jnp.int32, (chunk, 1), 0).astype(F32)

    for t in range(RET_HEADS_PER_STEP):
        log_gamma = jnp.full((1, 1), lg_ref[group * RET_HEADS_PER_STEP + t], F32)
        qk_cols = slice(t * RET_DK, (t + 1) * RET_DK)
        v_cols = slice(t * RET_DV, (t + 1) * RET_DV)
        q = q_ref[0, :, qk_cols]
        k = k_ref[0, :, qk_cols]
        v = v_ref[0, :, v_cols]
        inner_decay = jnp.where(rel >= 0, jnp.exp(jnp.maximum(rel, 0.0) * log_gamma), 0.0)
        cross_decay = jnp.exp((idx + 1.0) * log_gamma)
        state_decay = jnp.exp((chunk - 1.0 - idx) * log_gamma)
        chunk_decay = jnp.exp(float(chunk) * log_gamma)

        scores = lax.dot_general(q, k, (((1,), (1,)), ((), ())), preferred_element_type=F32) * inner_decay
        inner = jnp.dot(scores.astype(BF16), v, preferred_element_type=F32)
        state = state_ref[t]
        cross = jnp.dot(q, state.astype(BF16), preferred_element_type=F32) * cross_decay
        k_decayed = (k.astype(F32) * state_decay).astype(BF16)
        state_ref[t] = state * chunk_decay + lax.dot_general(
            k_decayed, v, (((0,), (0,)), ((), ())), preferred_element_type=F32)

        o = inner + cross
        mu = jnp.mean(o, axis=-1, keepdims=True)
        var = jnp.mean(jnp.square(o - mu), axis=-1, keepdims=True)
        o = (o - mu) * lax.rsqrt(var + EPS)
        o = o * gng_ref[:, v_cols] + gnb_ref[:, v_cols]
        o_ref[0, :, v_cols] = (jax.nn.silu(gate_ref[0, :, v_cols].astype(F32)) * o).astype(o_ref.dtype)


def _retention_core(hproj, gn_g, gn_b, batch, seq):
    hp = hproj.reshape(batch, seq, 6 * D_MODEL)
    log_gamma = jnp.log(1.0 - 2.0 ** (-5.0 - jnp.arange(RET_HEADS, dtype=F32)))
    qk_w = RET_HEADS_PER_STEP * RET_DK
    v_w = RET_HEADS_PER_STEP * RET_DV
    k_blk0 = D_MODEL // qk_w
    v_blk0 = 2 * D_MODEL // v_w
    g_blk0 = 4 * D_MODEL // v_w
    out = pl.pallas_call(
        _retention_kernel,
        out_shape=jax.ShapeDtypeStruct((batch, seq, 2 * D_MODEL), BF16),
        grid=(batch, RET_HEADS // RET_HEADS_PER_STEP, seq // RET_CHUNK),
        in_specs=[
            pl.BlockSpec(memory_space=pltpu.SMEM),
            pl.BlockSpec((1, RET_CHUNK, qk_w), lambda b, hg, c: (b, c, hg)),
            pl.BlockSpec((1, RET_CHUNK, qk_w), lambda b, hg, c: (b, c, k_blk0 + hg)),
            pl.BlockSpec((1, RET_CHUNK, v_w), lambda b, hg, c: (b, c, v_blk0 + hg)),
            pl.BlockSpec((1, RET_CHUNK, v_w), lambda b, hg, c: (b, c, g_blk0 + hg)),
            pl.BlockSpec((1, v_w), lambda b, hg, c: (0, hg)),
            pl.BlockSpec((1, v_w), lambda b, hg, c: (0, hg)),
        ],
        out_specs=pl.BlockSpec((1, RET_CHUNK, v_w), lambda b, hg, c: (b, c, hg)),
        scratch_shapes=[pltpu.VMEM((RET_HEADS_PER_STEP, RET_DK, RET_DV), F32)],
        compiler_params=_params("parallel", "parallel", "arbitrary"),
        name="retention_core",
    )(log_gamma, hp, hp, hp, hp, gn_g.reshape(1, 2 * D_MODEL), gn_b.reshape(1, 2 * D_MODEL))
    return out.reshape(batch * seq, 2 * D_MODEL)


def _swa_kernel(sink_ref, q_ref, kc_ref, vc_ref, kp_ref, vp_ref, o_ref):
    n = pl.program_id(1)
    blk = SWA_BLOCK
    lane = lax.broadcasted_iota(jnp.int32, (1, LANES), 1)
    low = lane < SWA_HD

    def both_halves(t, odd):
        t = t.astype(F32)
        keep = jnp.logical_not(low) if odd else low
        return jnp.where(keep, t, pltpu.roll(t, SWA_HD, axis=1)).astype(BF16)

    k_all = jnp.concatenate([kp_ref[0], kc_ref[0]], axis=0)
    v_all = jnp.concatenate([vp_ref[0], vc_ref[0]], axis=0)
    qi = lax.broadcasted_iota(jnp.int32, (blk, 2 * blk), 0)
    kj = lax.broadcasted_iota(jnp.int32, (blk, 2 * blk), 1)
    diff = qi + blk - kj
    first_key = jnp.where(n > 0, 0, blk)
    allowed = (diff >= 0) & (diff < SWA_WINDOW) & (kj >= first_key)

    for kv in range(SWA_KV_HEADS):
        kv_cols = slice((kv // 2) * LANES, (kv // 2 + 1) * LANES)
        k2 = both_halves(k_all[:, kv_cols], kv % 2)
        v2 = both_halves(v_all[:, kv_cols], kv % 2)
        outs = []
        for jh in range(SWA_GROUP):
            head = kv * SWA_GROUP + jh
            qp = q_ref[0, :, (head // 2) * LANES:(head // 2 + 1) * LANES].astype(F32)
            keep = jnp.logical_not(low) if head % 2 else low
            qh = jnp.where(keep, qp, 0.0).astype(BF16)
            sink = jnp.full((blk, 1), sink_ref[head] * LOG2_E, F32)
            s = lax.dot_general(qh, k2, (((1,), (1,)), ((), ())), preferred_element_type=F32)
            s = jnp.where(allowed, s, -jnp.inf)
            m = jnp.maximum(jnp.max(s, axis=-1, keepdims=True), sink)
            e = jnp.exp2(s - m)
            denom = jnp.sum(e, axis=-1, keepdims=True) + jnp.exp2(sink - m)
            ov = jnp.dot(e.astype(BF16), v2, preferred_element_type=F32)
            outs.append(ov / denom)
        for jp in range(SWA_GROUP // 2):
            c = kv * (SWA_GROUP // 2) + jp
            o_ref[0, :, c * LANES:(c + 1) * LANES] = jnp.where(low, outs[2 * jp], outs[2 * jp + 1]).astype(
                o_ref.dtype)


def _swa_core(hproj, sinks, batch, seq):
    width = hproj.shape[1]
    hp = hproj.reshape(batch, seq, width)
    q_width = SWA_HEADS * SWA_HD
    kv_width = SWA_KV_HEADS * SWA_HD
    k_blk = q_width // kv_width
    v_blk = k_blk + 1
    prev = lambda n: jnp.maximum(n - 1, 0)
    out = pl.pallas_call(
        _swa_kernel,
        out_shape=jax.ShapeDtypeStruct((batch, seq, q_width), BF16),
        grid=(batch, seq // SWA_BLOCK),
        in_specs=[
            pl.BlockSpec(memory_space=pltpu.SMEM),
            pl.BlockSpec((1, SWA_BLOCK, q_width), lambda b, n: (b, n, 0)),
            pl.BlockSpec((1, SWA_BLOCK, kv_width), lambda b, n: (b, n, k_blk)),
            pl.BlockSpec((1, SWA_BLOCK, kv_width), lambda b, n: (b, n, v_blk)),
            pl.BlockSpec((1, SWA_BLOCK, kv_width), lambda b, n: (b, prev(n), k_blk)),
            pl.BlockSpec((1, SWA_BLOCK, kv_width), lambda b, n: (b, prev(n), v_blk)),
        ],
        out_specs=pl.BlockSpec((1, SWA_BLOCK, q_width), lambda b, n: (b, n, 0)),
        compiler_params=_params("parallel", "parallel"),
        name="swa_core",
    )(sinks, hp, hp, hp, hp, hp)
    return out.reshape(batch * seq, q_width)


def _retention_mixer(h, ln, pos, w_in, gn_g, gn_b, w_out, batch, seq):
    inv_freq = ROPE_THETA ** (-jnp.linspace(0.0, 1.0, RET_DK // 2, dtype=F32))
    hproj = _proj_in(
        h, ln[0:1], pos, inv_freq.reshape(1, LANES), w_in, jnp.zeros((w_in.shape[1],), F32),
        col_tile=RET_COL_TILE, mode="pairs", q_cols=D_MODEL, k_cols=D_MODEL, q_scale=1.0,
        k_scale=RET_DK ** -0.5)
    gated = _retention_core(hproj, gn_g, gn_b, batch, seq)
    return _proj_out(gated, w_out, jnp.zeros((D_MODEL,), F32), ln[1:2], h)


def _swa_mixer(h, ln, pos, w_in, b_in, sinks, w_out, b_out, batch, seq):
    inv_freq = ROPE_THETA ** (-jnp.arange(0, SWA_HD, 2, dtype=F32) / SWA_HD)
    freq_row = jnp.tile(inv_freq, LANES // (SWA_HD // 2)).reshape(1, LANES)
    hproj = _proj_in(
        h, ln[0:1], pos, freq_row, w_in, b_in,
        col_tile=SWA_COL_TILE, mode="lane", q_cols=SWA_HEADS * SWA_HD, k_cols=SWA_KV_HEADS * SWA_HD,
        q_scale=LOG2_E * SWA_HD ** -0.5, k_scale=1.0)
    attn = _swa_core(hproj, sinks, batch, seq)
    return _proj_out(attn, w_out, b_out, ln[1:2], h)


def kernel(x, positions, ln_ffn1, ln_mix, ln_ffn2, ffn_w_in, ffn_w_out, pool_w, pool_scale, ret_w_in,
           ret_gn_g, ret_gn_b, ret_w_out, swa_w_in, swa_b_in, swa_sinks, swa_w_out, swa_b_out):
    batch, seq, _ = x.shape
    h = x.reshape(batch * seq, D_MODEL)
    pos = positions.reshape(batch * seq, 1)
    for i in range(DEPTH):
        h = _ffn(h, ln_ffn1[i], ffn_w_in, ffn_w_out, i, 0)
        kind, j = i % N_MIXERS, i // N_MIXERS
        if kind == 0:
            h = _pool_mixer(h, ln_mix[i], pool_w[j], pool_scale[j], batch, seq)
        elif kind == 1:
            h = _retention_mixer(h, ln_mix[i], pos, ret_w_in[j], ret_gn_g[j], ret_gn_b[j], ret_w_out[j],
                                 batch, seq)
        else:
            h = _swa_mixer(h, ln_mix[i], pos, swa_w_in[j], swa_b_in[j], swa_sinks[j], swa_w_out[j],
                           swa_b_out[j], batch, seq)
        h = _ffn(h, ln_ffn2[i], ffn_w_in, ffn_w_out, i, 1)
    return h.reshape(batch, seq, D_MODEL)
```

```python
import functools

import jax
import jax.numpy as jnp
from jax import lax
from jax.experimental import pallas as pl
from jax.experimental.pallas import tpu as pltpu

F32 = jnp.float32
BF16 = jnp.bfloat16

D_MODEL = 2048
DEPTH = 4
N_MIXERS = 3
EPS = 1e-6
D_FF = 5504
POOL_WINDOWS = (2, 4, 8, 16)
POOL_GROUP = D_MODEL // len(POOL_WINDOWS)
POOL_HALO = 16
RET_HEADS = 8
RET_DK = D_MODEL // RET_HEADS
RET_DV = 2 * D_MODEL // RET_HEADS
SWA_HEADS = 32
SWA_KV_HEADS = 4
SWA_HD = 64
SWA_GROUP = SWA_HEADS // SWA_KV_HEADS
SWA_WINDOW = 128
SWA_BLOCK = 128
ROPE_THETA = 10000.0
LOG2_E = 1.4426950408889634

LANES = 128
VMEM_LIMIT_BYTES = 58 * 1024 * 1024

ROW_TILE = 1024
NORM_ROWS = 128
H_SLOTS = 4
FF_TILE = 512
PROJ_OUT_K = 1024
POOL_ROWS = 512
POOL_SUB = 128
RET_COL_TILE = 1024
SWA_COL_TILE = 512
RET_HEADS_PER_STEP = 8
RET_CHUNK = 256


def _rms(x, g):
    return x * lax.rsqrt(jnp.mean(x * x, axis=-1, keepdims=True) + EPS) * g


def _params(*sem):
    return pltpu.CompilerParams(dimension_semantics=sem, vmem_limit_bytes=VMEM_LIMIT_BYTES)


def _norm_rows_to(dst_ref, src_ref, g, rows):
    def body(i, carry):
        r = pl.ds(pl.multiple_of(i * NORM_ROWS, NORM_ROWS), NORM_ROWS)
        dst_ref[r, :] = _rms(src_ref[r, :], g).astype(dst_ref.dtype)
        return carry
    lax.fori_loop(0, rows // NORM_ROWS, body, 0)


def _residual_norm_rows(o_ref, h_ref, g, rows):
    def body(i, carry):
        r = pl.ds(pl.multiple_of(i * NORM_ROWS, NORM_ROWS), NORM_ROWS)
        o_ref[r, :] = h_ref[r, :] + _rms(o_ref[r, :], g)
        return carry
    lax.fori_loop(0, rows // NORM_ROWS, body, 0)


def _ff_offset(f, base=0):
    return LANES * (base // LANES + jnp.minimum(f * (FF_TILE // LANES), (D_FF - FF_TILE) // LANES))


def _ffn_kernel(h_hbm, g1_ref, g2_ref, wg_ref, wu_ref, wo_ref, o_ref, xn_ref, hbuf_ref, hsem):
    i = pl.program_id(0)
    f = pl.program_id(1)
    last = pl.num_programs(1) - 1
    rows = o_ref.shape[0]
    pieces = rows // NORM_ROWS
    ahead = H_SLOTS - 1

    def h_copy(tile, p):
        src = h_hbm.at[pl.ds(tile * rows + p * NORM_ROWS, NORM_ROWS), :]
        return pltpu.make_async_copy(src, hbuf_ref.at[p % H_SLOTS], hsem.at[p % H_SLOTS])

    def start_head(tile):
        for p in range(ahead):
            h_copy(tile, p).start()

    def stream_h(use_piece, then_head_of_next_tile=False):
        for p in range(pieces):
            nxt = p + ahead
            if nxt < pieces:
                h_copy(i, nxt).start()
            elif then_head_of_next_tile:
                @pl.when(i + 1 < pl.num_programs(0))
                def _(nxt=nxt):
                    h_copy(i + 1, nxt - pieces).start()
            h_copy(i, p).wait()
            use_piece(slice(p * NORM_ROWS, (p + 1) * NORM_ROWS), hbuf_ref[p % H_SLOTS])

    @pl.when(f == 0)
    def _():
        @pl.when(i == 0)
        def _():
            start_head(i)

        g1 = g1_ref[...]

        def to_xn(r, hp):
            xn_ref[r, :] = _rms(hp, g1).astype(xn_ref.dtype)
            o_ref[r, :] = jnp.zeros((NORM_ROWS, o_ref.shape[1]), o_ref.dtype)
        stream_h(to_xn)

    @pl.when(f == last)
    def _():
        start_head(i)

    xn = xn_ref[...]
    gate = jnp.dot(xn, wg_ref[...].astype(BF16), preferred_element_type=F32)
    up = jnp.dot(xn, wu_ref[...].astype(BF16), preferred_element_type=F32)
    repeated = f * FF_TILE - _ff_offset(f)
    fresh = lax.broadcasted_iota(jnp.int32, (1, FF_TILE), 1) >= repeated
    hid = jnp.where(fresh, jax.nn.silu(gate) * up, 0.0).astype(BF16)
    o_ref[...] += jnp.dot(hid, wo_ref[...].astype(BF16), preferred_element_type=F32)

    @pl.when(f == last)
    def _():
        g2 = 0.5 * g2_ref[...]

        def residual(r, hp):
            o_ref[r, :] = hp + _rms(o_ref[r, :], g2)
        stream_h(residual, then_head_of_next_tile=True)


def _ffn(h, ln, w_in_all, w_out_all, layer, half):
    m = h.shape[0]
    n_ff = pl.cdiv(D_FF, FF_TILE)
    assert n_ff > 1 and ROW_TILE % (H_SLOTS * NORM_ROWS) == 0
    elem = pl.Element
    return pl.pallas_call(
        _ffn_kernel,
        out_shape=jax.ShapeDtypeStruct((m, D_MODEL), F32),
        grid=(m // ROW_TILE, n_ff),
        in_specs=[
            pl.BlockSpec(memory_space=pl.ANY),
            pl.BlockSpec((1, D_MODEL), lambda i, f: (0, 0)),
            pl.BlockSpec((1, D_MODEL), lambda i, f: (0, 0)),
            pl.BlockSpec((None, None, elem(D_MODEL), elem(FF_TILE)),
                         lambda i, f: (layer, half, 0, _ff_offset(f))),
            pl.BlockSpec((None, None, elem(D_MODEL), elem(FF_TILE)),
                         lambda i, f: (layer, half, 0, _ff_offset(f, D_FF))),
            pl.BlockSpec((None, None, elem(FF_TILE), elem(D_MODEL)),
                         lambda i, f: (layer, half, _ff_offset(f), 0)),
        ],
        out_specs=pl.BlockSpec((ROW_TILE, D_MODEL), lambda i, f: (i, 0)),
        scratch_shapes=[pltpu.VMEM((ROW_TILE, D_MODEL), BF16),
                        pltpu.VMEM((H_SLOTS, NORM_ROWS, D_MODEL), F32),
                        pltpu.SemaphoreType.DMA((H_SLOTS,))],
        compiler_params=_params("arbitrary", "arbitrary"),
        name="ffn",
    )(h, ln[0:1], ln[1:2], w_in_all, w_in_all, w_out_all)


def _pool_kernel(h_ref, halo_ref, gin_ref, gout_ref, band_ref, w_ref, scale_ref, o_ref):
    t = pl.program_id(1)
    rows = h_ref.shape[1]
    h = h_ref[0]
    gin = gin_ref[...]
    u = _rms(h, gin)
    has_prev = (t > 0).astype(F32)
    halo = _rms(halo_ref[0], gin) * has_prev
    lead = jnp.zeros((POOL_SUB - POOL_HALO, halo.shape[1]), F32)
    hist = jnp.concatenate([lead, halo, u], axis=0)
    hist_hi = hist.astype(BF16)
    hist_lo = (hist - hist_hi.astype(F32)).astype(BF16)
    seq_pos = t * rows + lax.broadcasted_iota(jnp.int32, (rows, 1), 0)
    outs = []
    for g, win in enumerate(POOL_WINDOWS):
        cols = slice(g * POOL_GROUP, (g + 1) * POOL_GROUP)
        band = band_ref[g]
        parts = []
        for sb in range(rows // POOL_SUB):
            hrows = slice(sb * POOL_SUB, (sb + 2) * POOL_SUB)
            parts.append(jnp.dot(band, hist_hi[hrows, cols], preferred_element_type=F32)
                         + jnp.dot(band, hist_lo[hrows, cols], preferred_element_type=F32))
        others = jnp.concatenate(parts, axis=0)
        count = jnp.minimum(seq_pos + 1, win).astype(F32)
        own = u[:, cols]
        mixed = ((others + own) / count - own).astype(BF16)
        y = jnp.dot(mixed, w_ref[g], preferred_element_type=F32)
        outs.append(y * scale_ref[:, cols])
    y = jnp.concatenate(outs, axis=1)
    o_ref[0] = h + _rms(y, gout_ref[...])


def _pool_mixer(h, ln, w_group, scale, batch, seq):
    h3 = h.reshape(batch, seq, D_MODEL)
    halo_blocks = POOL_ROWS // POOL_HALO
    back = (jnp.arange(POOL_SUB)[:, None] + POOL_SUB) - jnp.arange(2 * POOL_SUB)[None, :]
    band = jnp.stack([(back >= 1) & (back < win) for win in POOL_WINDOWS]).astype(BF16)
    out = pl.pallas_call(
        _pool_kernel,
        out_shape=jax.ShapeDtypeStruct(h3.shape, F32),
        grid=(batch, seq // POOL_ROWS),
        in_specs=[
            pl.BlockSpec((1, POOL_ROWS, D_MODEL), lambda b, t: (b, t, 0)),
            pl.BlockSpec((1, POOL_HALO, D_MODEL),
                         lambda b, t: (b, jnp.maximum(t * halo_blocks - 1, 0), 0)),
            pl.BlockSpec((1, D_MODEL), lambda b, t: (0, 0)),
            pl.BlockSpec((1, D_MODEL), lambda b, t: (0, 0)),
            pl.BlockSpec(band.shape, lambda b, t: (0, 0, 0)),
            pl.BlockSpec((len(POOL_WINDOWS), POOL_GROUP, POOL_GROUP), lambda b, t: (0, 0, 0)),
            pl.BlockSpec((1, D_MODEL), lambda b, t: (0, 0)),
        ],
        out_specs=pl.BlockSpec((1, POOL_ROWS, D_MODEL), lambda b, t: (b, t, 0)),
        compiler_params=_params("parallel", "parallel"),
        name="pool_mixer",
    )(h3, h3, ln[0:1], ln[1:2], band, w_group.astype(BF16), scale.reshape(1, D_MODEL))
    return out.reshape(batch * seq, D_MODEL)


def _proj_in_kernel(h_ref, g_ref, pos_ref, freq_ref, w_ref, b_ref, o_ref, xn_ref, rot_a_ref, rot_b_ref,
                    *, mode, q_cols, k_cols, q_scale, k_scale):
    j = pl.program_id(1)
    rows, col_tile = o_ref.shape
    lane = lax.broadcasted_iota(jnp.int32, (1, LANES), 1)
    first_half = (lane % SWA_HD) < (SWA_HD // 2)

    @pl.when(j == 0)
    def _():
        _norm_rows_to(xn_ref, h_ref, g_ref[...], rows)
        ang = pos_ref[...].astype(F32) * freq_ref[...]
        cos = jnp.cos(ang)
        sin = jnp.sin(ang)
        if mode == "lane":
            sin = jnp.where(first_half, -sin, sin)
        rot_a_ref[0] = cos * q_scale
        rot_b_ref[0] = sin * q_scale
        rot_a_ref[1] = cos * k_scale
        rot_b_ref[1] = sin * k_scale
        rot_a_ref[2] = jnp.ones_like(cos)
        rot_b_ref[2] = jnp.zeros_like(sin)

    y = jnp.dot(xn_ref[...], w_ref[...].astype(BF16), preferred_element_type=F32) + b_ref[...]
    unit = 2 * LANES if mode == "pairs" else LANES
    outs = []
    for c in range(col_tile // unit):
        col = j * col_tile + c * unit
        kind = jnp.where(col < q_cols, 0, jnp.where(col < q_cols + k_cols, 1, 2))
        a = rot_a_ref[kind]
        b = rot_b_ref[kind]
        if mode == "pairs":
            x1 = y[:, c * unit:c * unit + LANES]
            x2 = y[:, c * unit + LANES:(c + 1) * unit]
            outs += [x1 * a - x2 * b, x2 * a + x1 * b]
        else:
            half = SWA_HD // 2
            xb = y[:, c * unit:(c + 1) * unit]
            partner = jnp.where(first_half, pltpu.roll(xb, LANES - half, axis=1), pltpu.roll(xb, half, axis=1))
            outs.append(xb * a + partner * b)
    o_ref[...] = jnp.concatenate(outs, axis=1).astype(o_ref.dtype)


def _proj_in(h, g, pos, freq_row, w, bias, *, col_tile, mode, q_cols, k_cols, q_scale, k_scale):
    m = h.shape[0]
    n = w.shape[1]
    kern = functools.partial(_proj_in_kernel, mode=mode, q_cols=q_cols, k_cols=k_cols,
                             q_scale=q_scale, k_scale=k_scale)
    return pl.pallas_call(
        kern,
        out_shape=jax.ShapeDtypeStruct((m, n), BF16),
        grid=(m // ROW_TILE, n // col_tile),
        in_specs=[
            pl.BlockSpec((ROW_TILE, D_MODEL), lambda i, j: (i, 0)),
            pl.BlockSpec((1, D_MODEL), lambda i, j: (0, 0)),
            pl.BlockSpec((ROW_TILE, 1), lambda i, j: (i, 0)),
            pl.BlockSpec((1, LANES), lambda i, j: (0, 0)),
            pl.BlockSpec((D_MODEL, col_tile), lambda i, j: (0, j)),
            pl.BlockSpec((1, col_tile), lambda i, j: (0, j)),
        ],
        out_specs=pl.BlockSpec((ROW_TILE, col_tile), lambda i, j: (i, j)),
        scratch_shapes=[pltpu.VMEM((ROW_TILE, D_MODEL), BF16),
                        pltpu.VMEM((3, ROW_TILE, LANES), F32),
                        pltpu.VMEM((3, ROW_TILE, LANES), F32)],
        compiler_params=_params("parallel", "arbitrary"),
        name="proj_in_" + mode,
    )(h, g, pos, freq_row, w, bias.reshape(1, n))


def _proj_out_kernel(x_ref, w_ref, b_ref, g_ref, h_ref, o_ref):
    k = pl.program_id(1)
    rows = h_ref.shape[0]

    def partial_product():
        return jnp.dot(x_ref[...], w_ref[...].astype(BF16), preferred_element_type=F32)

    @pl.when(k == 0)
    def _():
        o_ref[...] = partial_product() + b_ref[...]

    @pl.when(k > 0)
    def _():
        o_ref[...] += partial_product()

    @pl.when(k == pl.num_programs(1) - 1)
    def _():
        _residual_norm_rows(o_ref, h_ref, g_ref[...], rows)


def _proj_out(x, w, bias, g, h):
    m, kdim = x.shape
    return pl.pallas_call(
        _proj_out_kernel,
        out_shape=jax.ShapeDtypeStruct((m, D_MODEL), F32),
        grid=(m // ROW_TILE, kdim // PROJ_OUT_K),
        in_specs=[
            pl.BlockSpec((ROW_TILE, PROJ_OUT_K), lambda i, k: (i, k)),
            pl.BlockSpec((PROJ_OUT_K, D_MODEL), lambda i, k: (k, 0)),
            pl.BlockSpec((1, D_MODEL), lambda i, k: (0, 0)),
            pl.BlockSpec((1, D_MODEL), lambda i, k: (0, 0)),
            pl.BlockSpec((ROW_TILE, D_MODEL), lambda i, k: (i, 0)),
        ],
        out_specs=pl.BlockSpec((ROW_TILE, D_MODEL), lambda i, k: (i, 0)),
        compiler_params=_params("parallel", "arbitrary"),
        name="proj_out",
    )(x, w, bias.reshape(1, D_MODEL), g, h)


def _retention_kernel(lg_ref, q_ref, k_ref, v_ref, gate_ref, gng_ref, gnb_ref, o_ref, state_ref):
    group = pl.program_id(1)
    chunk = q_ref.shape[1]

    @pl.when(pl.program_id(2) == 0)
    def _():
        state_ref[...] = jnp.zeros_like(state_ref)

    qi = lax.broadcasted_iota(jnp.int32, (chunk, chunk), 0)
    kj = lax.broadcasted_iota(jnp.int32, (chunk, chunk), 1)
    rel = (qi - kj).astype(F32)
    idx = lax.broadcasted_iota(jnp.int32, (chunk, 1), 0).astype(F32)

    for t in range(RET_HEADS_PER_STEP):
        log_gamma = jnp.full((1, 1), lg_ref[group * RET_HEADS_PER_STEP + t], F32)
        qk_cols = slice(t * RET_DK, (t + 1) * RET_DK)
        v_cols = slice(t * RET_DV, (t + 1) * RET_DV)
        q = q_ref[0, :, qk_cols]
        k = k_ref[0, :, qk_cols]
        v = v_ref[0, :, v_cols]
        inner_decay = jnp.where(rel >= 0, jnp.exp(jnp.maximum(rel, 0.0) * log_gamma), 0.0)
        cross_decay = jnp.exp((idx + 1.0) * log_gamma)
        state_decay = jnp.exp((chunk - 1.0 - idx) * log_gamma)
        chunk_decay = jnp.exp(float(chunk) * log_gamma)

        scores = lax.dot_general(q, k, (((1,), (1,)), ((), ())), preferred_element_type=F32) * inner_decay
        inner = jnp.dot(scores.astype(BF16), v, preferred_element_type=F32)
        state = state_ref[t]
        cross = jnp.dot(q, state.astype(BF16), preferred_element_type=F32) * cross_decay
        k_decayed = (k.astype(F32) * state_decay).astype(BF16)
        state_ref[t] = state * chunk_decay + lax.dot_general(
            k_decayed, v, (((0,), (0,)), ((), ())), preferred_element_type=F32)

        o = inner + cross
        mu = jnp.mean(o, axis=-1, keepdims=True)
        var = jnp.mean(jnp.square(o - mu), axis=-1, keepdims=True)
        o = (o - mu) * lax.rsqrt(var + EPS)
        o = o * gng_ref[:, v_cols] + gnb_ref[:, v_cols]
        o_ref[0, :, v_cols] = (jax.nn.silu(gate_ref[0, :, v_cols].astype(F32)) * o).astype(o_ref.dtype)


def _retention_core(hproj, gn_g, gn_b, batch, seq):
    hp = hproj.reshape(batch, seq, 6 * D_MODEL)
    log_gamma = jnp.log(1.0 - 2.0 ** (-5.0 - jnp.arange(RET_HEADS, dtype=F32)))
    qk_w = RET_HEADS_PER_STEP * RET_DK
    v_w = RET_HEADS_PER_STEP * RET_DV
    k_blk0 = D_MODEL // qk_w
    v_blk0 = 2 * D_MODEL // v_w
    g_blk0 = 4 * D_MODEL // v_w
    out = pl.pallas_call(
        _retention_kernel,
        out_shape=jax.ShapeDtypeStruct((batch, seq, 2 * D_MODEL), BF16),
        grid=(batch, RET_HEADS // RET_HEADS_PER_STEP, seq // RET_CHUNK),
        in_specs=[
            pl.BlockSpec(memory_space=pltpu.SMEM),
            pl.BlockSpec((1, RET_CHUNK, qk_w), lambda b, hg, c: (b, c, hg)),
            pl.BlockSpec((1, RET_CHUNK, qk_w), lambda b, hg, c: (b, c, k_blk0 + hg)),
            pl.BlockSpec((1, RET_CHUNK, v_w), lambda b, hg, c: (b, c, v_blk0 + hg)),
            pl.BlockSpec((1, RET_CHUNK, v_w), lambda b, hg, c: (b, c, g_blk0 + hg)),
            pl.BlockSpec((1, v_w), lambda b, hg, c: (0, hg)),
            pl.BlockSpec((1, v_w), lambda b, hg, c: (0, hg)),
        ],
        out_specs=pl.BlockSpec((1, RET_CHUNK, v_w), lambda b, hg, c: (b, c, hg)),
        scratch_shapes=[pltpu.VMEM((RET_HEADS_PER_STEP, RET_DK, RET_DV), F32)],
        compiler_params=_params("parallel", "parallel", "arbitrary"),
        name="retention_core",
    )(log_gamma, hp, hp, hp, hp, gn_g.reshape(1, 2 * D_MODEL), gn_b.reshape(1, 2 * D_MODEL))
    return out.reshape(batch * seq, 2 * D_MODEL)


def _swa_kernel(sink_ref, q_ref, kc_ref, vc_ref, kp_ref, vp_ref, o_ref):
    n = pl.program_id(1)
    blk = SWA_BLOCK
    lane = lax.broadcasted_iota(jnp.int32, (1, LANES), 1)
    low = lane < SWA_HD

    def both_halves(t, odd):
        t = t.astype(F32)
        keep = jnp.logical_not(low) if odd else low
        return jnp.where(keep, t, pltpu.roll(t, SWA_HD, axis=1)).astype(BF16)

    k_all = jnp.concatenate([kp_ref[0], kc_ref[0]], axis=0)
    v_all = jnp.concatenate([vp_ref[0], vc_ref[0]], axis=0)
    qi = lax.broadcasted_iota(jnp.int32, (blk, 2 * blk), 0)
    kj = lax.broadcasted_iota(jnp.int32, (blk, 2 * blk), 1)
    diff = qi + blk - kj
    first_key = jnp.where(n > 0, 0, blk)
    allowed = (diff >= 0) & (diff < SWA_WINDOW) & (kj >= first_key)

    for kv in range(SWA_KV_HEADS):
        kv_cols = slice((kv // 2) * LANES, (kv // 2 + 1) * LANES)
        k2 = both_halves(k_all[:, kv_cols], kv % 2)
        v2 = both_halves(v_all[:, kv_cols], kv % 2)
        outs = []
        for jh in range(SWA_GROUP):
            head = kv * SWA_GROUP + jh
            qp = q_ref[0, :, (head // 2) * LANES:(head // 2 + 1) * LANES].astype(F32)
            keep = jnp.logical_not(low) if head % 2 else low
            qh = jnp.where(keep, qp, 0.0).astype(BF16)
            sink = jnp.full((blk, 1), sink_ref[head] * LOG2_E, F32)
            s = lax.dot_general(qh, k2, (((1,), (1,)), ((), ())), preferred_element_type=F32)
            s = jnp.where(allowed, s, -jnp.inf)
            m = jnp.maximum(jnp.max(s, axis=-1, keepdims=True), sink)
            e = jnp.exp2(s - m)
            denom = jnp.sum(e, axis=-1, keepdims=True) + jnp.exp2(sink - m)
            ov = jnp.dot(e.astype(BF16), v2, preferred_element_type=F32)
            outs.append(ov / denom)
        for jp in range(SWA_GROUP // 2):
            c = kv * (SWA_GROUP // 2) + jp
            o_ref[0, :, c * LANES:(c + 1) * LANES] = jnp.where(low, outs[2 * jp], outs[2 * jp + 1]).astype(
                o_ref.dtype)


def _swa_core(hproj, sinks, batch, seq):
    width = hproj.shape[1]
    hp = hproj.reshape(batch, seq, width)
    q_width = SWA_HEADS * SWA_HD
    kv_width = SWA_KV_HEADS * SWA_HD
    k_blk = q_width // kv_width
    v_blk = k_blk + 1
    prev = lambda n: jnp.maximum(n - 1, 0)
    out = pl.pallas_call(
        _swa_kernel,
        out_shape=jax.ShapeDtypeStruct((batch, seq, q_width), BF16),
        grid=(batch, seq // SWA_BLOCK),
        in_specs=[
            pl.BlockSpec(memory_space=pltpu.SMEM),
            pl.BlockSpec((1, SWA_BLOCK, q_width), lambda b, n: (b, n, 0)),
            pl.BlockSpec((1, SWA_BLOCK, kv_width), lambda b, n: (b, n, k_blk)),
            pl.BlockSpec((1, SWA_BLOCK, kv_width), lambda b, n: (b, n, v_blk)),
            pl.BlockSpec((1, SWA_BLOCK, kv_width), lambda b, n: (b, prev(n), k_blk)),
            pl.BlockSpec((1, SWA_BLOCK, kv_width), lambda b, n: (b, prev(n), v_blk)),
        ],
        out_specs=pl.BlockSpec((1, SWA_BLOCK, q_width), lambda b, n: (b, n, 0)),
        compiler_params=_params("parallel", "parallel"),
        name="swa_core",
    )(sinks, hp, hp, hp, hp, hp)
    return out.reshape(batch * seq, q_width)


def _retention_mixer(h, ln, pos, w_in, gn_g, gn_b, w_out, batch, seq):
    inv_freq = ROPE_THETA ** (-jnp.linspace(0.0, 1.0, RET_DK // 2, dtype=F32))
    hproj = _proj_in(
        h, ln[0:1], pos, inv_freq.reshape(1, LANES), w_in, jnp.zeros((w_in.shape[1],), F32),
        col_tile=RET_COL_TILE, mode="pairs", q_cols=D_MODEL, k_cols=D_MODEL, q_scale=1.0,
        k_scale=RET_DK ** -0.5)
    gated = _retention_core(hproj, gn_g, gn_b, batch, seq)
    return _proj_out(gated, w_out, jnp.zeros((D_MODEL,), F32), ln[1:2], h)


def _swa_mixer(h, ln, pos, w_in, b_in, sinks, w_out, b_out, batch, seq):
    inv_freq = ROPE_THETA ** (-jnp.arange(0, SWA_HD, 2, dtype=F32) / SWA_HD)
    freq_row = jnp.tile(inv_freq, LANES // (SWA_HD // 2)).reshape(1, LANES)
    hproj = _proj_in(
        h, ln[0:1], pos, freq_row, w_in, b_in,
        col_tile=SWA_COL_TILE, mode="lane", q_cols=SWA_HEADS * SWA_HD, k_cols=SWA_KV_HEADS * SWA_HD,
        q_scale=LOG2_E * SWA_HD ** -0.5, k_scale=1.0)
    attn = _swa_core(hproj, sinks, batch, seq)
    return _proj_out(attn, w_out, b_out, ln[1:2], h)


def kernel(x, positions, ln_ffn1, ln_mix, ln_ffn2, ffn_w_in, ffn_w_out, pool_w, pool_scale, ret_w_in,
           ret_gn_g, ret_gn_b, ret_w_out, swa_w_in, swa_b_in, swa_sinks, swa_w_out, swa_b_out):
    batch, seq, _ = x.shape
    h = x.reshape(batch * seq, D_MODEL)
    pos = positions.reshape(batch * seq, 1)
    for i in range(DEPTH):
        h = _ffn(h, ln_ffn1[i], ffn_w_in, ffn_w_out, i, 0)
        kind, j = i % N_MIXERS, i // N_MIXERS
        if kind == 0:
            h = _pool_mixer(h, ln_mix[i], pool_w[j], pool_scale[j], batch, seq)
        elif kind == 1:
            h = _retention_mixer(h, ln_mix[i], pos, ret_w_in[j], ret_gn_g[j], ret_gn_b[j], ret_w_out[j],
                                 batch, seq)
        else:
            h = _swa_mixer(h, ln_mix[i], pos, swa_w_in[j], swa_b_in[j], swa_sinks[j], swa_w_out[j],
                           swa_b_out[j], batch, seq)
        h = _ffn(h, ln_ffn2[i], ffn_w_in, ffn_w_out, i, 1)
    return h.reshape(batch, seq, D_MODEL)
```

```python
import functools

import jax
import jax.numpy as jnp
from jax import lax
from jax.experimental import pallas as pl
from jax.experimental.pallas import tpu as pltpu

F32 = jnp.float32
BF16 = jnp.bfloat16

D_MODEL = 2048
DEPTH = 4
N_MIXERS = 3
EPS = 1e-6
D_FF = 5504
POOL_WINDOWS = (2, 4, 8, 16)
POOL_GROUP = D_MODEL // len(POOL_WINDOWS)
POOL_HALO = 16
RET_HEADS = 8
RET_DK = D_MODEL // RET_HEADS
RET_DV = 2 * D_MODEL // RET_HEADS
SWA_HEADS = 32
SWA_KV_HEADS = 4
SWA_HD = 64
SWA_GROUP = SWA_HEADS // SWA_KV_HEADS
SWA_WINDOW = 128
SWA_BLOCK = 128
ROPE_THETA = 10000.0
LOG2_E = 1.4426950408889634

LANES = 128
VMEM_LIMIT_BYTES = 58 * 1024 * 1024

ROW_TILE = 1024
NORM_ROWS = 128
FF_TILE = 256
PROJ_OUT_K = 1024
POOL_ROWS = 512
POOL_SUB = 128
W_SLOTS = 3
RET_COL_TILE = 1024
SWA_COL_TILE = 512
RET_HEADS_PER_STEP = 8
RET_CHUNK = 256


def _rms(x, g):
    return x * lax.rsqrt(jnp.mean(x * x, axis=-1, keepdims=True) + EPS) * g


def _params(*sem):
    return pltpu.CompilerParams(dimension_semantics=sem, vmem_limit_bytes=VMEM_LIMIT_BYTES)


def _norm_rows_to(dst_ref, src_ref, g, rows, zero_ref=None):
    def body(i, carry):
        r = pl.ds(pl.multiple_of(i * NORM_ROWS, NORM_ROWS), NORM_ROWS)
        dst_ref[r, :] = _rms(src_ref[r, :], g).astype(dst_ref.dtype)
        if zero_ref is not None:
            zero_ref[r, :] = jnp.zeros((NORM_ROWS, zero_ref.shape[1]), zero_ref.dtype)
        return carry
    lax.fori_loop(0, rows // NORM_ROWS, body, 0)


def _residual_norm_rows(o_ref, h_ref, g, rows):
    def body(i, carry):
        r = pl.ds(pl.multiple_of(i * NORM_ROWS, NORM_ROWS), NORM_ROWS)
        o_ref[r, :] = h_ref[r, :] + _rms(o_ref[r, :], g)
        return carry
    lax.fori_loop(0, rows // NORM_ROWS, body, 0)


def _ff_offset(f, base=0):
    return LANES * (base // LANES + jnp.minimum(f * (FF_TILE // LANES), (D_FF - FF_TILE) // LANES))


def _ffn_kernel(h_ref, g1_ref, g2_ref, wg_ref, wu_ref, wo_ref, o_ref, xn_ref):
    f = pl.program_id(1)
    rows = h_ref.shape[0]

    @pl.when(f == 0)
    def _():
        _norm_rows_to(xn_ref, h_ref, g1_ref[...], rows, zero_ref=o_ref)

    xn = xn_ref[...]
    gate = jnp.dot(xn, wg_ref[...].astype(BF16), preferred_element_type=F32)
    up = jnp.dot(xn, wu_ref[...].astype(BF16), preferred_element_type=F32)
    repeated = f * FF_TILE - _ff_offset(f)
    fresh = lax.broadcasted_iota(jnp.int32, (1, FF_TILE), 1) >= repeated
    hid = jnp.where(fresh, jax.nn.silu(gate) * up, 0.0).astype(BF16)
    o_ref[...] += jnp.dot(hid, wo_ref[...].astype(BF16), preferred_element_type=F32)

    @pl.when(f == pl.num_programs(1) - 1)
    def _():
        _residual_norm_rows(o_ref, h_ref, 0.5 * g2_ref[...], rows)


def _ffn(h, ln, w_in_all, w_out_all, layer, half):
    m = h.shape[0]
    elem = pl.Element
    return pl.pallas_call(
        _ffn_kernel,
        out_shape=jax.ShapeDtypeStruct((m, D_MODEL), F32),
        grid=(m // ROW_TILE, pl.cdiv(D_FF, FF_TILE)),
        in_specs=[
            pl.BlockSpec((ROW_TILE, D_MODEL), lambda i, f: (i, 0)),
            pl.BlockSpec((1, D_MODEL), lambda i, f: (0, 0)),
            pl.BlockSpec((1, D_MODEL), lambda i, f: (0, 0)),
            pl.BlockSpec((None, None, elem(D_MODEL), elem(FF_TILE)),
                         lambda i, f: (layer, half, 0, _ff_offset(f))),
            pl.BlockSpec((None, None, elem(D_MODEL), elem(FF_TILE)),
                         lambda i, f: (layer, half, 0, _ff_offset(f, D_FF))),
            pl.BlockSpec((None, None, elem(FF_TILE), elem(D_MODEL)),
                         lambda i, f: (layer, half, _ff_offset(f), 0)),
        ],
        out_specs=pl.BlockSpec((ROW_TILE, D_MODEL), lambda i, f: (i, 0)),
        scratch_shapes=[pltpu.VMEM((ROW_TILE, D_MODEL), BF16)],
        compiler_params=_params("parallel", "arbitrary"),
        name="ffn",
    )(h, ln[0:1], ln[1:2], w_in_all, w_in_all, w_out_all)


def _pool_kernel(h_ref, halo_ref, gin_ref, gout_ref, band_ref, w_ref, scale_ref, o_ref):
    t = pl.program_id(1)
    rows = h_ref.shape[1]
    h = h_ref[0]
    gin = gin_ref[...]
    u = _rms(h, gin)
    has_prev = (t > 0).astype(F32)
    halo = _rms(halo_ref[0], gin) * has_prev
    hist = jnp.concatenate([halo, u], axis=0)
    hist_hi = hist.astype(BF16)
    hist_lo = (hist - hist_hi.astype(F32)).astype(BF16)
    seq_pos = t * rows + lax.broadcasted_iota(jnp.int32, (rows, 1), 0)
    outs = []
    for g, win in enumerate(POOL_WINDOWS):
        cols = slice(g * POOL_GROUP, (g + 1) * POOL_GROUP)
        band = band_ref[g]
        parts = []
        for sb in range(rows // POOL_SUB):
            hrows = slice(sb * POOL_SUB, (sb + 1) * POOL_SUB + POOL_HALO)
            parts.append(jnp.dot(band, hist_hi[hrows, cols], preferred_element_type=F32)
                         + jnp.dot(band, hist_lo[hrows, cols], preferred_element_type=F32))
        others = jnp.concatenate(parts, axis=0)
        count = jnp.minimum(seq_pos + 1, win).astype(F32)
        own = u[:, cols]
        mixed = ((others + own) / count - own).astype(BF16)
        y = jnp.dot(mixed, w_ref[g], preferred_element_type=F32)
        outs.append(y * scale_ref[:, cols])
    y = jnp.concatenate(outs, axis=1)
    o_ref[0] = h + _rms(y, gout_ref[...])


def _pool_mixer(h, ln, w_group, scale, batch, seq):
    h3 = h.reshape(batch, seq, D_MODEL)
    halo_blocks = POOL_ROWS // POOL_HALO
    back = (jnp.arange(POOL_SUB)[:, None] + POOL_HALO) - jnp.arange(POOL_SUB + POOL_HALO)[None, :]
    band = jnp.stack([(back >= 1) & (back < win) for win in POOL_WINDOWS]).astype(BF16)
    out = pl.pallas_call(
        _pool_kernel,
        out_shape=jax.ShapeDtypeStruct(h3.shape, F32),
        grid=(batch, seq // POOL_ROWS),
        in_specs=[
            pl.BlockSpec((1, POOL_ROWS, D_MODEL), lambda b, t: (b, t, 0)),
            pl.BlockSpec((1, POOL_HALO, D_MODEL),
                         lambda b, t: (b, jnp.maximum(t * halo_blocks - 1, 0), 0)),
            pl.BlockSpec((1, D_MODEL), lambda b, t: (0, 0)),
            pl.BlockSpec((1, D_MODEL), lambda b, t: (0, 0)),
            pl.BlockSpec(band.shape, lambda b, t: (0, 0, 0)),
            pl.BlockSpec((len(POOL_WINDOWS), POOL_GROUP, POOL_GROUP), lambda b, t: (0, 0, 0)),
            pl.BlockSpec((1, D_MODEL), lambda b, t: (0, 0)),
        ],
        out_specs=pl.BlockSpec((1, POOL_ROWS, D_MODEL), lambda b, t: (b, t, 0)),
        compiler_params=_params("parallel", "parallel"),
        name="pool_mixer",
    )(h3, h3, ln[0:1], ln[1:2], band, w_group.astype(BF16), scale.reshape(1, D_MODEL))
    return out.reshape(batch * seq, D_MODEL)


def _proj_in_kernel(h_ref, g_ref, pos_ref, freq_ref, w_hbm, b_ref, o_ref, xn_ref, rot_a_ref, rot_b_ref,
                    wbuf_ref, wsem, *, mode, q_cols, k_cols, q_scale, k_scale):
    j = pl.program_id(1)
    rows, col_tile = o_ref.shape
    lane = lax.broadcasted_iota(jnp.int32, (1, LANES), 1)
    first_half = (lane % SWA_HD) < (SWA_HD // 2)

    @pl.when(j == 0)
    def _():
        _norm_rows_to(xn_ref, h_ref, g_ref[...], rows)
        ang = pos_ref[...].astype(F32) * freq_ref[...]
        cos = jnp.cos(ang)
        sin = jnp.sin(ang)
        if mode == "lane":
            sin = jnp.where(first_half, -sin, sin)
        rot_a_ref[0] = cos * q_scale
        rot_b_ref[0] = sin * q_scale
        rot_a_ref[1] = cos * k_scale
        rot_b_ref[1] = sin * k_scale
        rot_a_ref[2] = jnp.ones_like(cos)
        rot_b_ref[2] = jnp.zeros_like(sin)

    n_col = pl.num_programs(1)
    step = pl.program_id(0) * n_col + j
    n_steps = pl.num_programs(0) * n_col

    def w_copy(s):
        src = w_hbm.at[:, pl.ds(pl.multiple_of((s % n_col) * col_tile, LANES), col_tile)]
        return pltpu.make_async_copy(src, wbuf_ref.at[s % W_SLOTS], wsem.at[s % W_SLOTS])

    @pl.when(step == 0)
    def _():
        for s in range(W_SLOTS - 1):
            w_copy(s).start()

    @pl.when(step + W_SLOTS - 1 < n_steps)
    def _():
        w_copy(step + W_SLOTS - 1).start()

    w_copy(step).wait()
    w = wbuf_ref[step % W_SLOTS]
    y = jnp.dot(xn_ref[...], w.astype(BF16), preferred_element_type=F32) + b_ref[...]
    unit = 2 * LANES if mode == "pairs" else LANES
    outs = []
    for c in range(col_tile // unit):
        col = j * col_tile + c * unit
        kind = jnp.where(col < q_cols, 0, jnp.where(col < q_cols + k_cols, 1, 2))
        a = rot_a_ref[kind]
        b = rot_b_ref[kind]
        if mode == "pairs":
            x1 = y[:, c * unit:c * unit + LANES]
            x2 = y[:, c * unit + LANES:(c + 1) * unit]
            outs += [x1 * a - x2 * b, x2 * a + x1 * b]
        else:
            half = SWA_HD // 2
            xb = y[:, c * unit:(c + 1) * unit]
            partner = jnp.where(first_half, pltpu.roll(xb, LANES - half, axis=1), pltpu.roll(xb, half, axis=1))
            outs.append(xb * a + partner * b)
    o_ref[...] = jnp.concatenate(outs, axis=1).astype(o_ref.dtype)


def _proj_in(h, g, pos, freq_row, w, bias, *, col_tile, mode, q_cols, k_cols, q_scale, k_scale):
    m = h.shape[0]
    n = w.shape[1]
    assert (m // ROW_TILE) * (n // col_tile) >= W_SLOTS
    kern = functools.partial(_proj_in_kernel, mode=mode, q_cols=q_cols, k_cols=k_cols,
                             q_scale=q_scale, k_scale=k_scale)
    return pl.pallas_call(
        kern,
        out_shape=jax.ShapeDtypeStruct((m, n), BF16),
        grid=(m // ROW_TILE, n // col_tile),
        in_specs=[
            pl.BlockSpec((ROW_TILE, D_MODEL), lambda i, j: (i, 0)),
            pl.BlockSpec((1, D_MODEL), lambda i, j: (0, 0)),
            pl.BlockSpec((ROW_TILE, 1), lambda i, j: (i, 0)),
            pl.BlockSpec((1, LANES), lambda i, j: (0, 0)),
            pl.BlockSpec(memory_space=pl.ANY),
            pl.BlockSpec((1, col_tile), lambda i, j: (0, j)),
        ],
        out_specs=pl.BlockSpec((ROW_TILE, col_tile), lambda i, j: (i, j)),
        scratch_shapes=[pltpu.VMEM((ROW_TILE, D_MODEL), BF16),
                        pltpu.VMEM((3, ROW_TILE, LANES), F32),
                        pltpu.VMEM((3, ROW_TILE, LANES), F32),
                        pltpu.VMEM((W_SLOTS, D_MODEL, col_tile), F32),
                        pltpu.SemaphoreType.DMA((W_SLOTS,))],
        compiler_params=_params("arbitrary", "arbitrary"),
        name="proj_in_" + mode,
    )(h, g, pos, freq_row, w, bias.reshape(1, n))


def _proj_out_kernel(x_ref, w_ref, b_ref, g_ref, h_ref, o_ref):
    k = pl.program_id(1)
    rows = h_ref.shape[0]

    def partial_product():
        return jnp.dot(x_ref[...], w_ref[...].astype(BF16), preferred_element_type=F32)

    @pl.when(k == 0)
    def _():
        o_ref[...] = partial_product() + b_ref[...]

    @pl.when(k > 0)
    def _():
        o_ref[...] += partial_product()

    @pl.when(k == pl.num_programs(1) - 1)
    def _():
        _residual_norm_rows(o_ref, h_ref, g_ref[...], rows)


def _proj_out(x, w, bias, g, h):
    m, kdim = x.shape
    return pl.pallas_call(
        _proj_out_kernel,
        out_shape=jax.ShapeDtypeStruct((m, D_MODEL), F32),
        grid=(m // ROW_TILE, kdim // PROJ_OUT_K),
        in_specs=[
            pl.BlockSpec((ROW_TILE, PROJ_OUT_K), lambda i, k: (i, k)),
            pl.BlockSpec((PROJ_OUT_K, D_MODEL), lambda i, k: (k, 0)),
            pl.BlockSpec((1, D_MODEL), lambda i, k: (0, 0)),
            pl.BlockSpec((1, D_MODEL), lambda i, k: (0, 0)),
            pl.BlockSpec((ROW_TILE, D_MODEL), lambda i, k: (i, 0)),
        ],
        out_specs=pl.BlockSpec((ROW_TILE, D_MODEL), lambda i, k: (i, 0)),
        compiler_params=_params("parallel", "arbitrary"),
        name="proj_out",
    )(x, w, bias.reshape(1, D_MODEL), g, h)


def _retention_kernel(lg_ref, q_ref, k_ref, v_ref, gate_ref, gng_ref, gnb_ref, o_ref, state_ref):
    group = pl.program_id(1)
    chunk = q_ref.shape[1]

    @pl.when(pl.program_id(2) == 0)
    def _():
        state_ref[...] = jnp.zeros_like(state_ref)

    qi = lax.broadcasted_iota(jnp.int32, (chunk, chunk), 0)
    kj = lax.broadcasted_iota(jnp.int32, (chunk, chunk), 1)
    rel = (qi - kj).astype(F32)
    idx = lax.broadcasted_iota(jnp.int32, (chunk, 1), 0).astype(F32)

    for t in range(RET_HEADS_PER_STEP):
        log_gamma = jnp.full((1, 1), lg_ref[group * RET_HEADS_PER_STEP + t], F32)
        qk_cols = slice(t * RET_DK, (t + 1) * RET_DK)
        v_cols = slice(t * RET_DV, (t + 1) * RET_DV)
        q = q_ref[0, :, qk_cols]
        k = k_ref[0, :, qk_cols]
        v = v_ref[0, :, v_cols]
        inner_decay = jnp.where(rel >= 0, jnp.exp(jnp.maximum(rel, 0.0) * log_gamma), 0.0)
        cross_decay = jnp.exp((idx + 1.0) * log_gamma)
        state_decay = jnp.exp((chunk - 1.0 - idx) * log_gamma)
        chunk_decay = jnp.exp(float(chunk) * log_gamma)

        scores = lax.dot_general(q, k, (((1,), (1,)), ((), ())), preferred_element_type=F32) * inner_decay
        inner = jnp.dot(scores.astype(BF16), v, preferred_element_type=F32)
        state = state_ref[t]
        cross = jnp.dot(q, state.astype(BF16), preferred_element_type=F32) * cross_decay
        k_decayed = (k.astype(F32) * state_decay).astype(BF16)
        state_ref[t] = state * chunk_decay + lax.dot_general(
            k_decayed, v, (((0,), (0,)), ((), ())), preferred_element_type=F32)

        o = inner + cross
        mu = jnp.mean(o, axis=-1, keepdims=True)
        var = jnp.mean(jnp.square(o - mu), axis=-1, keepdims=True)
        o = (o - mu) * lax.rsqrt(var + EPS)
        o = o * gng_ref[:, v_cols] + gnb_ref[:, v_cols]
        o_ref[0, :, v_cols] = (jax.nn.silu(gate_ref[0, :, v_cols].astype(F32)) * o).astype(o_ref.dtype)


def _retention_core(hproj, gn_g, gn_b, batch, seq):
    hp = hproj.reshape(batch, seq, 6 * D_MODEL)
    log_gamma = jnp.log(1.0 - 2.0 ** (-5.0 - jnp.arange(RET_HEADS, dtype=F32)))
    qk_w = RET_HEADS_PER_STEP * RET_DK
    v_w = RET_HEADS_PER_STEP * RET_DV
    k_blk0 = D_MODEL // qk_w
    v_blk0 = 2 * D_MODEL // v_w
    g_blk0 = 4 * D_MODEL // v_w
    out = pl.pallas_call(
        _retention_kernel,
        out_shape=jax.ShapeDtypeStruct((batch, seq, 2 * D_MODEL), BF16),
        grid=(batch, RET_HEADS // RET_HEADS_PER_STEP, seq // RET_CHUNK),
        in_specs=[
            pl.BlockSpec(memory_space=pltpu.SMEM),
            pl.BlockSpec((1, RET_CHUNK, qk_w), lambda b, hg, c: (b, c, hg)),
            pl.BlockSpec((1, RET_CHUNK, qk_w), lambda b, hg, c: (b, c, k_blk0 + hg)),
            pl.BlockSpec((1, RET_CHUNK, v_w), lambda b, hg, c: (b, c, v_blk0 + hg)),
            pl.BlockSpec((1, RET_CHUNK, v_w), lambda b, hg, c: (b, c, g_blk0 + hg)),
            pl.BlockSpec((1, v_w), lambda b, hg, c: (0, hg)),
            pl.BlockSpec((1, v_w), lambda b, hg, c: (0, hg)),
        ],
        out_specs=pl.BlockSpec((1, RET_CHUNK, v_w), lambda b, hg, c: (b, c, hg)),
        scratch_shapes=[pltpu.VMEM((RET_HEADS_PER_STEP, RET_DK, RET_DV), F32)],
        compiler_params=_params("parallel", "parallel", "arbitrary"),
        name="retention_core",
    )(log_gamma, hp, hp, hp, hp, gn_g.reshape(1, 2 * D_MODEL), gn_b.reshape(1, 2 * D_MODEL))
    return out.reshape(batch * seq, 2 * D_MODEL)


def _swa_kernel(sink_ref, q_ref, kc_ref, vc_ref, kp_ref, vp_ref, o_ref):
    n = pl.program_id(1)
    blk = SWA_BLOCK
    lane = lax.broadcasted_iota(jnp.int32, (1, LANES), 1)
    low = lane < SWA_HD

    def both_halves(t, odd):
        t = t.astype(F32)
        keep = jnp.logical_not(low) if odd else low
        return jnp.where(keep, t, pltpu.roll(t, SWA_HD, axis=1)).astype(BF16)

    k_all = jnp.concatenate([kp_ref[0], kc_ref[0]], axis=0)
    v_all = jnp.concatenate([vp_ref[0], vc_ref[0]], axis=0)
    qi = lax.broadcasted_iota(jnp.int32, (blk, 2 * blk), 0)
    kj = lax.broadcasted_iota(jnp.int32, (blk, 2 * blk), 1)
    diff = qi + blk - kj
    first_key = jnp.where(n > 0, 0, blk)
    allowed = (diff >= 0) & (diff < SWA_WINDOW) & (kj >= first_key)

    for kv in range(SWA_KV_HEADS):
        kv_cols = slice((kv // 2) * LANES, (kv // 2 + 1) * LANES)
        k2 = both_halves(k_all[:, kv_cols], kv % 2)
        v2 = both_halves(v_all[:, kv_cols], kv % 2)
        outs = []
        for jh in range(SWA_GROUP):
            head = kv * SWA_GROUP + jh
            qp = q_ref[0, :, (head // 2) * LANES:(head // 2 + 1) * LANES].astype(F32)
            keep = jnp.logical_not(low) if head % 2 else low
            qh = jnp.where(keep, qp, 0.0).astype(BF16)
            sink = jnp.full((blk, 1), sink_ref[head] * LOG2_E, F32)
            s = lax.dot_general(qh, k2, (((1,), (1,)), ((), ())), preferred_element_type=F32)
            s = jnp.where(allowed, s, -jnp.inf)
            m = jnp.maximum(jnp.max(s, axis=-1, keepdims=True), sink)
            e = jnp.exp2(s - m)
            denom = jnp.sum(e, axis=-1, keepdims=True) + jnp.exp2(sink - m)
            ov = jnp.dot(e.astype(BF16), v2, preferred_element_type=F32)
            outs.append(ov / denom)
        for jp in range(SWA_GROUP // 2):
            c = kv * (SWA_GROUP // 2) + jp
            o_ref[0, :, c * LANES:(c + 1) * LANES] = jnp.where(low, outs[2 * jp], outs[2 * jp + 1]).astype(
                o_ref.dtype)


def _swa_core(hproj, sinks, batch, seq):
    width = hproj.shape[1]
    hp = hproj.reshape(batch, seq, width)
    q_width = SWA_HEADS * SWA_HD
    kv_width = SWA_KV_HEADS * SWA_HD
    k_blk = q_width // kv_width
    v_blk = k_blk + 1
    prev = lambda n: jnp.maximum(n - 1, 0)
    out = pl.pallas_call(
        _swa_kernel,
        out_shape=jax.ShapeDtypeStruct((batch, seq, q_width), BF16),
        grid=(batch, seq // SWA_BLOCK),
        in_specs=[
            pl.BlockSpec(memory_space=pltpu.SMEM),
            pl.BlockSpec((1, SWA_BLOCK, q_width), lambda b, n: (b, n, 0)),
            pl.BlockSpec((1, SWA_BLOCK, kv_width), lambda b, n: (b, n, k_blk)),
            pl.BlockSpec((1, SWA_BLOCK, kv_width), lambda b, n: (b, n, v_blk)),
            pl.BlockSpec((1, SWA_BLOCK, kv_width), lambda b, n: (b, prev(n), k_blk)),
            pl.BlockSpec((1, SWA_BLOCK, kv_width), lambda b, n: (b, prev(n), v_blk)),
        ],
        out_specs=pl.BlockSpec((1, SWA_BLOCK, q_width), lambda b, n: (b, n, 0)),
        compiler_params=_params("parallel", "parallel"),
        name="swa_core",
    )(sinks, hp, hp, hp, hp, hp)
    return out.reshape(batch * seq, q_width)


def _retention_mixer(h, ln, pos, w_in, gn_g, gn_b, w_out, batch, seq):
    inv_freq = ROPE_THETA ** (-jnp.linspace(0.0, 1.0, RET_DK // 2, dtype=F32))
    hproj = _proj_in(
        h, ln[0:1], pos, inv_freq.reshape(1, LANES), w_in, jnp.zeros((w_in.shape[1],), F32),
        col_tile=RET_COL_TILE, mode="pairs", q_cols=D_MODEL, k_cols=D_MODEL, q_scale=1.0,
        k_scale=RET_DK ** -0.5)
    gated = _retention_core(hproj, gn_g, gn_b, batch, seq)
    return _proj_out(gated, w_out, jnp.zeros((D_MODEL,), F32), ln[1:2], h)


def _swa_mixer(h, ln, pos, w_in, b_in, sinks, w_out, b_out, batch, seq):
    inv_freq = ROPE_THETA ** (-jnp.arange(0, SWA_HD, 2, dtype=F32) / SWA_HD)
    freq_row = jnp.tile(inv_freq, LANES // (SWA_HD // 2)).reshape(1, LANES)
    hproj = _proj_in(
        h, ln[0:1], pos, freq_row, w_in, b_in,
        col_tile=SWA_COL_TILE, mode="lane", q_cols=SWA_HEADS * SWA_HD, k_cols=SWA_KV_HEADS * SWA_HD,
        q_scale=LOG2_E * SWA_HD ** -0.5, k_scale=1.0)
    attn = _swa_core(hproj, sinks, batch, seq)
    return _proj_out(attn, w_out, b_out, ln[1:2], h)


def kernel(x, positions, ln_ffn1, ln_mix, ln_ffn2, ffn_w_in, ffn_w_out, pool_w, pool_scale, ret_w_in,
           ret_gn_g, ret_gn_b, ret_w_out, swa_w_in, swa_b_in, swa_sinks, swa_w_out, swa_b_out):
    batch, seq, _ = x.shape
    h = x.reshape(batch * seq, D_MODEL)
    pos = positions.reshape(batch * seq, 1)
    for i in range(DEPTH):
        h = _ffn(h, ln_ffn1[i], ffn_w_in, ffn_w_out, i, 0)
        kind, j = i % N_MIXERS, i // N_MIXERS
        if kind == 0:
            h = _pool_mixer(h, ln_mix[i], pool_w[j], pool_scale[j], batch, seq)
        elif kind == 1:
            h = _retention_mixer(h, ln_mix[i], pos, ret_w_in[j], ret_gn_g[j], ret_gn_b[j], ret_w_out[j],
                                 batch, seq)
        else:
            h = _swa_mixer(h, ln_mix[i], pos, swa_w_in[j], swa_b_in[j], swa_sinks[j], swa_w_out[j],
                           swa_b_out[j], batch, seq)
        h = _ffn(h, ln_ffn2[i], ffn_w_in, ffn_w_out, i, 1)
    return h.reshape(batch, seq, D_MODEL)
```

```python
import functools

import jax
import jax.numpy as jnp
from jax import lax
from jax.experimental import pallas as pl
from jax.experimental.pallas import tpu as pltpu

F32 = jnp.float32
BF16 = jnp.bfloat16

D_MODEL = 2048
DEPTH = 4
N_MIXERS = 3
EPS = 1e-6
D_FF = 5504
POOL_WINDOWS = (2, 4, 8, 16)
POOL_GROUP = D_MODEL // len(POOL_WINDOWS)
POOL_HALO = 16
RET_HEADS = 8
RET_DK = D_MODEL // RET_HEADS
RET_DV = 2 * D_MODEL // RET_HEADS
SWA_HEADS = 32
SWA_KV_HEADS = 4
SWA_HD = 64
SWA_GROUP = SWA_HEADS // SWA_KV_HEADS
SWA_WINDOW = 128
SWA_BLOCK = 128
ROPE_THETA = 10000.0
LOG2_E = 1.4426950408889634

LANES = 128
VMEM_LIMIT_BYTES = 58 * 1024 * 1024

ROW_TILE = 1024
NORM_ROWS = 128
FF_TILE = 256
PROJ_OUT_K = 1024
POOL_ROWS = 512
POOL_SUB = 128
W_SLOTS = 3
RET_COL_TILE = 1024
SWA_COL_TILE = 512
RET_HEADS_PER_STEP = 8
RET_CHUNK = 256


def _rms(x, g):
    return x * lax.rsqrt(jnp.mean(x * x, axis=-1, keepdims=True) + EPS) * g


def _params(*sem):
    return pltpu.CompilerParams(dimension_semantics=sem, vmem_limit_bytes=VMEM_LIMIT_BYTES)


def _norm_rows_to(dst_ref, src_ref, g, rows, zero_ref=None):
    def body(i, carry):
        r = pl.ds(pl.multiple_of(i * NORM_ROWS, NORM_ROWS), NORM_ROWS)
        dst_ref[r, :] = _rms(src_ref[r, :], g).astype(dst_ref.dtype)
        if zero_ref is not None:
            zero_ref[r, :] = jnp.zeros((NORM_ROWS, zero_ref.shape[1]), zero_ref.dtype)
        return carry
    lax.fori_loop(0, rows // NORM_ROWS, body, 0)


def _residual_norm_rows(o_ref, h_ref, g, rows):
    def body(i, carry):
        r = pl.ds(pl.multiple_of(i * NORM_ROWS, NORM_ROWS), NORM_ROWS)
        o_ref[r, :] = h_ref[r, :] + _rms(o_ref[r, :], g)
        return carry
    lax.fori_loop(0, rows // NORM_ROWS, body, 0)


def _ff_offset(f, base=0):
    return LANES * (base // LANES + jnp.minimum(f * (FF_TILE // LANES), (D_FF - FF_TILE) // LANES))


def _ffn_kernel(h_ref, g1_ref, g2_ref, wg_ref, wu_ref, wo_ref, o_ref, xn_ref):
    f = pl.program_id(1)
    rows = h_ref.shape[0]

    @pl.when(f == 0)
    def _():
        _norm_rows_to(xn_ref, h_ref, g1_ref[...], rows, zero_ref=o_ref)

    xn = xn_ref[...]
    gate = jnp.dot(xn, wg_ref[...].astype(BF16), preferred_element_type=F32)
    up = jnp.dot(xn, wu_ref[...].astype(BF16), preferred_element_type=F32)
    repeated = f * FF_TILE - _ff_offset(f)
    fresh = lax.broadcasted_iota(jnp.int32, (1, FF_TILE), 1) >= repeated
    hid = jnp.where(fresh, jax.nn.silu(gate) * up, 0.0).astype(BF16)
    o_ref[...] += jnp.dot(hid, wo_ref[...].astype(BF16), preferred_element_type=F32)

    @pl.when(f == pl.num_programs(1) - 1)
    def _():
        _residual_norm_rows(o_ref, h_ref, 0.5 * g2_ref[...], rows)


def _ffn(h, ln, w_in_all, w_out_all, layer, half):
    m = h.shape[0]
    elem = pl.Element
    return pl.pallas_call(
        _ffn_kernel,
        out_shape=jax.ShapeDtypeStruct((m, D_MODEL), F32),
        grid=(m // ROW_TILE, pl.cdiv(D_FF, FF_TILE)),
        in_specs=[
            pl.BlockSpec((ROW_TILE, D_MODEL), lambda i, f: (i, 0)),
            pl.BlockSpec((1, D_MODEL), lambda i, f: (0, 0)),
            pl.BlockSpec((1, D_MODEL), lambda i, f: (0, 0)),
            pl.BlockSpec((None, None, elem(D_MODEL), elem(FF_TILE)),
                         lambda i, f: (layer, half, 0, _ff_offset(f))),
            pl.BlockSpec((None, None, elem(D_MODEL), elem(FF_TILE)),
                         lambda i, f: (layer, half, 0, _ff_offset(f, D_FF))),
            pl.BlockSpec((None, None, elem(FF_TILE), elem(D_MODEL)),
                         lambda i, f: (layer, half, _ff_offset(f), 0)),
        ],
        out_specs=pl.BlockSpec((ROW_TILE, D_MODEL), lambda i, f: (i, 0)),
        scratch_shapes=[pltpu.VMEM((ROW_TILE, D_MODEL), BF16)],
        compiler_params=_params("parallel", "arbitrary"),
        name="ffn",
    )(h, ln[0:1], ln[1:2], w_in_all, w_in_all, w_out_all)


def _pool_kernel(h_ref, halo_ref, gin_ref, gout_ref, band_ref, w_ref, scale_ref, o_ref):
    t = pl.program_id(1)
    rows = h_ref.shape[1]
    h = h_ref[0]
    gin = gin_ref[...]
    u = _rms(h, gin)
    has_prev = (t > 0).astype(F32)
    halo = _rms(halo_ref[0], gin) * has_prev
    hist = jnp.concatenate([halo, u], axis=0)
    hist_hi = hist.astype(BF16)
    hist_lo = (hist - hist_hi.astype(F32)).astype(BF16)
    seq_pos = t * rows + lax.broadcasted_iota(jnp.int32, (rows, 1), 0)
    outs = []
    for g, win in enumerate(POOL_WINDOWS):
        cols = slice(g * POOL_GROUP, (g + 1) * POOL_GROUP)
        band = band_ref[g]
        parts = []
        for sb in range(rows // POOL_SUB):
            hrows = slice(sb * POOL_SUB, (sb + 1) * POOL_SUB + POOL_HALO)
            parts.append(jnp.dot(band, hist_hi[hrows, cols], preferred_element_type=F32)
                         + jnp.dot(band, hist_lo[hrows, cols], preferred_element_type=F32))
        others = jnp.concatenate(parts, axis=0)
        count = jnp.minimum(seq_pos + 1, win).astype(F32)
        own = u[:, cols]
        mixed = ((others + own) / count - own).astype(BF16)
        y = jnp.dot(mixed, w_ref[g], preferred_element_type=F32)
        outs.append(y * scale_ref[:, cols])
    y = jnp.concatenate(outs, axis=1)
    o_ref[0] = h + _rms(y, gout_ref[...])


def _pool_mixer(h, ln, w_group, scale, batch, seq):
    h3 = h.reshape(batch, seq, D_MODEL)
    halo_blocks = POOL_ROWS // POOL_HALO
    back = (jnp.arange(POOL_SUB)[:, None] + POOL_HALO) - jnp.arange(POOL_SUB + POOL_HALO)[None, :]
    band = jnp.stack([(back >= 1) & (back < win) for win in POOL_WINDOWS]).astype(BF16)
    out = pl.pallas_call(
        _pool_kernel,
        out_shape=jax.ShapeDtypeStruct(h3.shape, F32),
        grid=(batch, seq // POOL_ROWS),
        in_specs=[
            pl.BlockSpec((1, POOL_ROWS, D_MODEL), lambda b, t: (b, t, 0)),
            pl.BlockSpec((1, POOL_HALO, D_MODEL),
                         lambda b, t: (b, jnp.maximum(t * halo_blocks - 1, 0), 0)),
            pl.BlockSpec((1, D_MODEL), lambda b, t: (0, 0)),
            pl.BlockSpec((1, D_MODEL), lambda b, t: (0, 0)),
            pl.BlockSpec(band.shape, lambda b, t: (0, 0, 0)),
            pl.BlockSpec((len(POOL_WINDOWS), POOL_GROUP, POOL_GROUP), lambda b, t: (0, 0, 0)),
            pl.BlockSpec((1, D_MODEL), lambda b, t: (0, 0)),
        ],
        out_specs=pl.BlockSpec((1, POOL_ROWS, D_MODEL), lambda b, t: (b, t, 0)),
        compiler_params=_params("parallel", "parallel"),
        name="pool_mixer",
    )(h3, h3, ln[0:1], ln[1:2], band, w_group.astype(BF16), scale.reshape(1, D_MODEL))
    return out.reshape(batch * seq, D_MODEL)


def _proj_in_kernel(h_hbm, g_ref, pos_ref, freq_ref, w_hbm, b_ref, o_ref, xn_ref, rot_a_ref, rot_b_ref,
                    wbuf_ref, wsem, hbuf_ref, hsem, *, mode, q_cols, k_cols, q_scale, k_scale):
    j = pl.program_id(1)
    rows, col_tile = o_ref.shape
    lane = lax.broadcasted_iota(jnp.int32, (1, LANES), 1)
    first_half = (lane % SWA_HD) < (SWA_HD // 2)

    i = pl.program_id(0)

    def h_copy(tile):
        return pltpu.make_async_copy(h_hbm.at[pl.ds(tile * rows, rows), :], hbuf_ref, hsem.at[0])

    @pl.when(j == 0)
    def _():
        @pl.when(i == 0)
        def _():
            h_copy(i).start()
        h_copy(i).wait()
        _norm_rows_to(xn_ref, hbuf_ref, g_ref[...], rows)

    @pl.when((j == 1) & (i + 1 < pl.num_programs(0)))
    def _():
        h_copy(i + 1).start()

    @pl.when(j == 0)
    def _():
        ang = pos_ref[...].astype(F32) * freq_ref[...]
        cos = jnp.cos(ang)
        sin = jnp.sin(ang)
        if mode == "lane":
            sin = jnp.where(first_half, -sin, sin)
        rot_a_ref[0] = cos * q_scale
        rot_b_ref[0] = sin * q_scale
        rot_a_ref[1] = cos * k_scale
        rot_b_ref[1] = sin * k_scale
        rot_a_ref[2] = jnp.ones_like(cos)
        rot_b_ref[2] = jnp.zeros_like(sin)

    n_col = pl.num_programs(1)
    step = pl.program_id(0) * n_col + j
    n_steps = pl.num_programs(0) * n_col

    def w_copy(s):
        src = w_hbm.at[:, pl.ds(pl.multiple_of((s % n_col) * col_tile, LANES), col_tile)]
        return pltpu.make_async_copy(src, wbuf_ref.at[s % W_SLOTS], wsem.at[s % W_SLOTS])

    @pl.when(step == 0)
    def _():
        for s in range(W_SLOTS - 1):
            w_copy(s).start()

    @pl.when(step + W_SLOTS - 1 < n_steps)
    def _():
        w_copy(step + W_SLOTS - 1).start()

    w_copy(step).wait()
    w = wbuf_ref[step % W_SLOTS]
    y = jnp.dot(xn_ref[...], w.astype(BF16), preferred_element_type=F32) + b_ref[...]
    unit = 2 * LANES if mode == "pairs" else LANES
    outs = []
    for c in range(col_tile // unit):
        col = j * col_tile + c * unit
        kind = jnp.where(col < q_cols, 0, jnp.where(col < q_cols + k_cols, 1, 2))
        a = rot_a_ref[kind]
        b = rot_b_ref[kind]
        if mode == "pairs":
            x1 = y[:, c * unit:c * unit + LANES]
            x2 = y[:, c * unit + LANES:(c + 1) * unit]
            outs += [x1 * a - x2 * b, x2 * a + x1 * b]
        else:
            half = SWA_HD // 2
            xb = y[:, c * unit:(c + 1) * unit]
            partner = jnp.where(first_half, pltpu.roll(xb, LANES - half, axis=1), pltpu.roll(xb, half, axis=1))
            outs.append(xb * a + partner * b)
    o_ref[...] = jnp.concatenate(outs, axis=1).astype(o_ref.dtype)


def _proj_in(h, g, pos, freq_row, w, bias, *, col_tile, mode, q_cols, k_cols, q_scale, k_scale):
    m = h.shape[0]
    n = w.shape[1]
    assert (m // ROW_TILE) * (n // col_tile) >= W_SLOTS
    assert n // col_tile >= 2
    kern = functools.partial(_proj_in_kernel, mode=mode, q_cols=q_cols, k_cols=k_cols,
                             q_scale=q_scale, k_scale=k_scale)
    return pl.pallas_call(
        kern,
        out_shape=jax.ShapeDtypeStruct((m, n), BF16),
        grid=(m // ROW_TILE, n // col_tile),
        in_specs=[
            pl.BlockSpec(memory_space=pl.ANY),
            pl.BlockSpec((1, D_MODEL), lambda i, j: (0, 0)),
            pl.BlockSpec((ROW_TILE, 1), lambda i, j: (i, 0)),
            pl.BlockSpec((1, LANES), lambda i, j: (0, 0)),
            pl.BlockSpec(memory_space=pl.ANY),
            pl.BlockSpec((1, col_tile), lambda i, j: (0, j)),
        ],
        out_specs=pl.BlockSpec((ROW_TILE, col_tile), lambda i, j: (i, j)),
        scratch_shapes=[pltpu.VMEM((ROW_TILE, D_MODEL), BF16),
                        pltpu.VMEM((3, ROW_TILE, LANES), F32),
                        pltpu.VMEM((3, ROW_TILE, LANES), F32),
                        pltpu.VMEM((W_SLOTS, D_MODEL, col_tile), F32),
                        pltpu.SemaphoreType.DMA((W_SLOTS,)),
                        pltpu.VMEM((ROW_TILE, D_MODEL), F32),
                        pltpu.SemaphoreType.DMA((1,))],
        compiler_params=_params("arbitrary", "arbitrary"),
        name="proj_in_" + mode,
    )(h, g, pos, freq_row, w, bias.reshape(1, n))


def _proj_out_kernel(x_ref, w_ref, b_ref, g_ref, h_ref, o_ref):
    k = pl.program_id(1)
    rows = h_ref.shape[0]

    def partial_product():
        return jnp.dot(x_ref[...], w_ref[...].astype(BF16), preferred_element_type=F32)

    @pl.when(k == 0)
    def _():
        o_ref[...] = partial_product() + b_ref[...]

    @pl.when(k > 0)
    def _():
        o_ref[...] += partial_product()

    @pl.when(k == pl.num_programs(1) - 1)
    def _():
        _residual_norm_rows(o_ref, h_ref, g_ref[...], rows)


def _proj_out(x, w, bias, g, h):
    m, kdim = x.shape
    return pl.pallas_call(
        _proj_out_kernel,
        out_shape=jax.ShapeDtypeStruct((m, D_MODEL), F32),
        grid=(m // ROW_TILE, kdim // PROJ_OUT_K),
        in_specs=[
            pl.BlockSpec((ROW_TILE, PROJ_OUT_K), lambda i, k: (i, k)),
            pl.BlockSpec((PROJ_OUT_K, D_MODEL), lambda i, k: (k, 0)),
            pl.BlockSpec((1, D_MODEL), lambda i, k: (0, 0)),
            pl.BlockSpec((1, D_MODEL), lambda i, k: (0, 0)),
            pl.BlockSpec((ROW_TILE, D_MODEL), lambda i, k: (i, 0)),
        ],
        out_specs=pl.BlockSpec((ROW_TILE, D_MODEL), lambda i, k: (i, 0)),
        compiler_params=_params("parallel", "arbitrary"),
        name="proj_out",
    )(x, w, bias.reshape(1, D_MODEL), g, h)


def _retention_kernel(lg_ref, q_ref, k_ref, v_ref, gate_ref, gng_ref, gnb_ref, o_ref, state_ref):
    group = pl.program_id(1)
    chunk = q_ref.shape[1]

    @pl.when(pl.program_id(2) == 0)
    def _():
        state_ref[...] = jnp.zeros_like(state_ref)

    qi = lax.broadcasted_iota(jnp.int32, (chunk, chunk), 0)
    kj = lax.broadcasted_iota(jnp.int32, (chunk, chunk), 1)
    rel = (qi - kj).astype(F32)
    idx = lax.broadcasted_iota(jnp.int32, (chunk, 1), 0).astype(F32)

    for t in range(RET_HEADS_PER_STEP):
        log_gamma = jnp.full((1, 1), lg_ref[group * RET_HEADS_PER_STEP + t], F32)
        qk_cols = slice(t * RET_DK, (t + 1) * RET_DK)
        v_cols = slice(t * RET_DV, (t + 1) * RET_DV)
        q = q_ref[0, :, qk_cols]
        k = k_ref[0, :, qk_cols]
        v = v_ref[0, :, v_cols]
        inner_decay = jnp.where(rel >= 0, jnp.exp(jnp.maximum(rel, 0.0) * log_gamma), 0.0)
        cross_decay = jnp.exp((idx + 1.0) * log_gamma)
        state_decay = jnp.exp((chunk - 1.0 - idx) * log_gamma)
        chunk_decay = jnp.exp(float(chunk) * log_gamma)

        scores = lax.dot_general(q, k, (((1,), (1,)), ((), ())), preferred_element_type=F32) * inner_decay
        inner = jnp.dot(scores.astype(BF16), v, preferred_element_type=F32)
        state = state_ref[t]
        cross = jnp.dot(q, state.astype(BF16), preferred_element_type=F32) * cross_decay
        k_decayed = (k.astype(F32) * state_decay).astype(BF16)
        state_ref[t] = state * chunk_decay + lax.dot_general(
            k_decayed, v, (((0,), (0,)), ((), ())), preferred_element_type=F32)

        o = inner + cross
        mu = jnp.mean(o, axis=-1, keepdims=True)
        var = jnp.mean(jnp.square(o - mu), axis=-1, keepdims=True)
        o = (o - mu) * lax.rsqrt(var + EPS)
        o = o * gng_ref[:, v_cols] + gnb_ref[:, v_cols]
        o_ref[0, :, v_cols] = (jax.nn.silu(gate_ref[0, :, v_cols].astype(F32)) * o).astype(o_ref.dtype)


def _retention_core(hproj, gn_g, gn_b, batch, seq):
    hp = hproj.reshape(batch, seq, 6 * D_MODEL)
    log_gamma = jnp.log(1.0 - 2.0 ** (-5.0 - jnp.arange(RET_HEADS, dtype=F32)))
    qk_w = RET_HEADS_PER_STEP * RET_DK
    v_w = RET_HEADS_PER_STEP * RET_DV
    k_blk0 = D_MODEL // qk_w
    v_blk0 = 2 * D_MODEL // v_w
    g_blk0 = 4 * D_MODEL // v_w
    out = pl.pallas_call(
        _retention_kernel,
        out_shape=jax.ShapeDtypeStruct((batch, seq, 2 * D_MODEL), BF16),
        grid=(batch, RET_HEADS // RET_HEADS_PER_STEP, seq // RET_CHUNK),
        in_specs=[
            pl.BlockSpec(memory_space=pltpu.SMEM),
            pl.BlockSpec((1, RET_CHUNK, qk_w), lambda b, hg, c: (b, c, hg)),
            pl.BlockSpec((1, RET_CHUNK, qk_w), lambda b, hg, c: (b, c, k_blk0 + hg)),
            pl.BlockSpec((1, RET_CHUNK, v_w), lambda b, hg, c: (b, c, v_blk0 + hg)),
            pl.BlockSpec((1, RET_CHUNK, v_w), lambda b, hg, c: (b, c, g_blk0 + hg)),
            pl.BlockSpec((1, v_w), lambda b, hg, c: (0, hg)),
            pl.BlockSpec((1, v_w), lambda b, hg, c: (0, hg)),
        ],
        out_specs=pl.BlockSpec((1, RET_CHUNK, v_w), lambda b, hg, c: (b, c, hg)),
        scratch_shapes=[pltpu.VMEM((RET_HEADS_PER_STEP, RET_DK, RET_DV), F32)],
        compiler_params=_params("parallel", "parallel", "arbitrary"),
        name="retention_core",
    )(log_gamma, hp, hp, hp, hp, gn_g.reshape(1, 2 * D_MODEL), gn_b.reshape(1, 2 * D_MODEL))
    return out.reshape(batch * seq, 2 * D_MODEL)


def _swa_kernel(sink_ref, q_ref, kc_ref, vc_ref, kp_ref, vp_ref, o_ref):
    n = pl.program_id(1)
    blk = SWA_BLOCK
    lane = lax.broadcasted_iota(jnp.int32, (1, LANES), 1)
    low = lane < SWA_HD

    def both_halves(t, odd):
        t = t.astype(F32)
        keep = jnp.logical_not(low) if odd else low
        return jnp.where(keep, t, pltpu.roll(t, SWA_HD, axis=1)).astype(BF16)

    k_all = jnp.concatenate([kp_ref[0], kc_ref[0]], axis=0)
    v_all = jnp.concatenate([vp_ref[0], vc_ref[0]], axis=0)
    qi = lax.broadcasted_iota(jnp.int32, (blk, 2 * blk), 0)
    kj = lax.broadcasted_iota(jnp.int32, (blk, 2 * blk), 1)
    diff = qi + blk - kj
    first_key = jnp.where(n > 0, 0, blk)
    allowed = (diff >= 0) & (diff < SWA_WINDOW) & (kj >= first_key)

    for kv in range(SWA_KV_HEADS):
        kv_cols = slice((kv // 2) * LANES, (kv // 2 + 1) * LANES)
        k2 = both_halves(k_all[:, kv_cols], kv % 2)
        v2 = both_halves(v_all[:, kv_cols], kv % 2)
        outs = []
        for jh in range(SWA_GROUP):
            head = kv * SWA_GROUP + jh
            qp = q_ref[0, :, (head // 2) * LANES:(head // 2 + 1) * LANES].astype(F32)
            keep = jnp.logical_not(low) if head % 2 else low
            qh = jnp.where(keep, qp, 0.0).astype(BF16)
            sink = jnp.full((blk, 1), sink_ref[head] * LOG2_E, F32)
            s = lax.dot_general(qh, k2, (((1,), (1,)), ((), ())), preferred_element_type=F32)
            s = jnp.where(allowed, s, -jnp.inf)
            m = jnp.maximum(jnp.max(s, axis=-1, keepdims=True), sink)
            e = jnp.exp2(s - m)
            denom = jnp.sum(e, axis=-1, keepdims=True) + jnp.exp2(sink - m)
            ov = jnp.dot(e.astype(BF16), v2, preferred_element_type=F32)
            outs.append(ov / denom)
        for jp in range(SWA_GROUP // 2):
            c = kv * (SWA_GROUP // 2) + jp
            o_ref[0, :, c * LANES:(c + 1) * LANES] = jnp.where(low, outs[2 * jp], outs[2 * jp + 1]).astype(
                o_ref.dtype)


def _swa_core(hproj, sinks, batch, seq):
    width = hproj.shape[1]
    hp = hproj.reshape(batch, seq, width)
    q_width = SWA_HEADS * SWA_HD
    kv_width = SWA_KV_HEADS * SWA_HD
    k_blk = q_width // kv_width
    v_blk = k_blk + 1
    prev = lambda n: jnp.maximum(n - 1, 0)
    out = pl.pallas_call(
        _swa_kernel,
        out_shape=jax.ShapeDtypeStruct((batch, seq, q_width), BF16),
        grid=(batch, seq // SWA_BLOCK),
        in_specs=[
            pl.BlockSpec(memory_space=pltpu.SMEM),
            pl.BlockSpec((1, SWA_BLOCK, q_width), lambda b, n: (b, n, 0)),
            pl.BlockSpec((1, SWA_BLOCK, kv_width), lambda b, n: (b, n, k_blk)),
            pl.BlockSpec((1, SWA_BLOCK, kv_width), lambda b, n: (b, n, v_blk)),
            pl.BlockSpec((1, SWA_BLOCK, kv_width), lambda b, n: (b, prev(n), k_blk)),
            pl.BlockSpec((1, SWA_BLOCK, kv_width), lambda b, n: (b, prev(n), v_blk)),
        ],
        out_specs=pl.BlockSpec((1, SWA_BLOCK, q_width), lambda b, n: (b, n, 0)),
        compiler_params=_params("parallel", "parallel"),
        name="swa_core",
    )(sinks, hp, hp, hp, hp, hp)
    return out.reshape(batch * seq, q_width)


def _retention_mixer(h, ln, pos, w_in, gn_g, gn_b, w_out, batch, seq):
    inv_freq = ROPE_THETA ** (-jnp.linspace(0.0, 1.0, RET_DK // 2, dtype=F32))
    hproj = _proj_in(
        h, ln[0:1], pos, inv_freq.reshape(1, LANES), w_in, jnp.zeros((w_in.shape[1],), F32),
        col_tile=RET_COL_TILE, mode="pairs", q_cols=D_MODEL, k_cols=D_MODEL, q_scale=1.0,
        k_scale=RET_DK ** -0.5)
    gated = _retention_core(hproj, gn_g, gn_b, batch, seq)
    return _proj_out(gated, w_out, jnp.zeros((D_MODEL,), F32), ln[1:2], h)


def _swa_mixer(h, ln, pos, w_in, b_in, sinks, w_out, b_out, batch, seq):
    inv_freq = ROPE_THETA ** (-jnp.arange(0, SWA_HD, 2, dtype=F32) / SWA_HD)
    freq_row = jnp.tile(inv_freq, LANES // (SWA_HD // 2)).reshape(1, LANES)
    hproj = _proj_in(
        h, ln[0:1], pos, freq_row, w_in, b_in,
        col_tile=SWA_COL_TILE, mode="lane", q_cols=SWA_HEADS * SWA_HD, k_cols=SWA_KV_HEADS * SWA_HD,
        q_scale=LOG2_E * SWA_HD ** -0.5, k_scale=1.0)
    attn = _swa_core(hproj, sinks, batch, seq)
    return _proj_out(attn, w_out, b_out, ln[1:2], h)


def kernel(x, positions, ln_ffn1, ln_mix, ln_ffn2, ffn_w_in, ffn_w_out, pool_w, pool_scale, ret_w_in,
           ret_gn_g, ret_gn_b, ret_w_out, swa_w_in, swa_b_in, swa_sinks, swa_w_out, swa_b_out):
    batch, seq, _ = x.shape
    h = x.reshape(batch * seq, D_MODEL)
    pos = positions.reshape(batch * seq, 1)
    for i in range(DEPTH):
        h = _ffn(h, ln_ffn1[i], ffn_w_in, ffn_w_out, i, 0)
        kind, j = i % N_MIXERS, i // N_MIXERS
        if kind == 0:
            h = _pool_mixer(h, ln_mix[i], pool_w[j], pool_scale[j], batch, seq)
        elif kind == 1:
            h = _retention_mixer(h, ln_mix[i], pos, ret_w_in[j], ret_gn_g[j], ret_gn_b[j], ret_w_out[j],
                                 batch, seq)
        else:
            h = _swa_mixer(h, ln_mix[i], pos, swa_w_in[j], swa_b_in[j], swa_sinks[j], swa_w_out[j],
                           swa_b_out[j], batch, seq)
        h = _ffn(h, ln_ffn2[i], ffn_w_in, ffn_w_out, i, 1)
    return h.reshape(batch, seq, D_MODEL)
```
